```python
import jax
import jax.numpy as jnp
from jax import lax
import numpy as np

D_MODEL = 4096
BATCH = 1
SEQ = 16384
DEPTH = 2

GRID_W = 64
CTX_LEN = 256
D_RET = D_MODEL // 2
RET_HEAD_DIM = 256
N_RET_HEADS = D_RET // RET_HEAD_DIM
D_SC = D_MODEL - D_RET
SC_WIDTH = 3
RET_CHUNK = 128
ROPE_BASE = 10000.0
CONF_KERNEL = 31
FFN_HIDDEN = -(-8 * D_MODEL // (3 * 256)) * 256
N_MOD = 6
EPS = 1e-6
IN_COLS = 4 * D_RET + 3 * D_SC

kernel_name = 'hybrid_retention_shortconv_conformer_dit'


def rms_norm(x, g):
    xf = x.astype(jnp.float32)
    y = xf * lax.rsqrt(jnp.mean(xf * xf, axis=-1, keepdims=True) + EPS)
    return (y * g.astype(jnp.float32)).astype(x.dtype)


def layer_norm(x, g, b):
    xf = x.astype(jnp.float32)
    mu = jnp.mean(xf, axis=-1, keepdims=True)
    xc = xf - mu
    y = xc * lax.rsqrt(jnp.mean(xc * xc, axis=-1, keepdims=True) + EPS)
    return (y * g.astype(jnp.float32) + b.astype(jnp.float32)).astype(x.dtype)


def modulate(h, shift, scale):
    return h * (1 + scale) + shift


def to_heads(t):
    return t.reshape(t.shape[0], t.shape[1], N_RET_HEADS, RET_HEAD_DIM).transpose(0, 2, 1, 3)


def rope_2d(t, rows, cols):
    dh = t.shape[-1]
    n_freq = dh // 4
    freqs = ROPE_BASE ** (-jnp.arange(n_freq, dtype=jnp.float32) / n_freq)
    ang = jnp.concatenate([rows[:, None] * freqs, cols[:, None] * freqs], axis=-1)
    cos, sin = jnp.cos(ang), jnp.sin(ang)
    tf = t.astype(jnp.float32)
    t1, t2 = tf[..., : dh // 2], tf[..., dh // 2:]
    return jnp.concatenate([t1 * cos - t2 * sin, t1 * sin + t2 * cos], axis=-1).astype(t.dtype)


def context_state(k, v, log_gamma):
    lc = k.shape[2]
    j = jnp.arange(lc, dtype=jnp.float32)
    w = jnp.exp(log_gamma[:, None] * (lc - 1 - j))
    return jnp.einsum('bhjd,hj,bhje->bhde', k, w, v)


def retention_chunkwise(q, k, v, log_gamma, s0):
    b, h, seq_len, dk = q.shape
    dv = v.shape[-1]
    n = seq_len // RET_CHUNK
    pos = jnp.arange(RET_CHUNK, dtype=jnp.float32)
    lg = log_gamma[:, None]
    diff = pos[:, None] - pos[None, :]
    intra_decay = jnp.where(diff >= 0, jnp.exp(lg[:, :, None] * jnp.maximum(diff, 0.0)), 0.0)
    qc = q.reshape(b, h, n, RET_CHUNK, dk)
    kc = k.reshape(b, h, n, RET_CHUNK, dk)
    vc = v.reshape(b, h, n, RET_CHUNK, dv)
    scores = jnp.einsum('bhncd,bhnmd->bhncm', qc, kc) * intra_decay[None, :, None]
    intra = jnp.einsum('bhncm,bhnme->bhnce', scores, vc)
    q_in = qc * jnp.exp(lg * (pos + 1.0))[None, :, None, :, None]
    k_in = kc * jnp.exp(lg * (RET_CHUNK - 1.0 - pos))[None, :, None, :, None]
    chunk_decay = jnp.exp(log_gamma * RET_CHUNK)[None, :, None, None]

    def step(s, inp):
        qi, ki, vi = inp
        cross = jnp.einsum('bhcd,bhde->bhce', qi, s)
        s_new = chunk_decay * s + jnp.einsum('bhcd,bhce->bhde', ki, vi)
        return s_new, cross

    _, cross = lax.scan(step, s0, (jnp.moveaxis(q_in, 2, 0), jnp.moveaxis(k_in, 2, 0), jnp.moveaxis(vc, 2, 0)))
    out = intra + jnp.moveaxis(cross, 0, 2)
    return out.reshape(b, h, seq_len, dv)


def head_group_norm(y):
    yf = y.astype(jnp.float32)
    mu = jnp.mean(yf, axis=-1, keepdims=True)
    yc = yf - mu
    return yc * lax.rsqrt(jnp.mean(yc * yc, axis=-1, keepdims=True) + EPS)


def short_conv(u, w):
    seq_len = u.shape[1]
    pad = (SC_WIDTH - 1) // 2
    up = jnp.pad(u, ((0, 0), (pad, pad), (0, 0)))
    return sum(up[:, j:j + seq_len] * w[j] for j in range(SC_WIDTH))


def depthwise_conv(u, w):
    kw, ch = w.shape
    pad = (kw - 1) // 2
    return lax.conv_general_dilated(u, w[:, None, :].astype(u.dtype), window_strides=(1,), padding=[(pad, pad)], dimension_numbers=('NWC', 'WIO', 'NWC'), feature_group_count=ch)


def even_mixer(h, hc, w_in, log_decay, sc_w, w_out, rows, cols):
    b, seq_len, _ = h.shape
    proj = h @ w_in
    cuts = [D_RET, 2 * D_RET, 3 * D_RET, 4 * D_RET, 4 * D_RET + D_SC, 4 * D_RET + 2 * D_SC]
    q, k, v, g, bg, cg, xin = jnp.split(proj, cuts, axis=-1)
    qk_scale = RET_HEAD_DIM ** -0.5
    q = rope_2d(to_heads(q), rows, cols)
    k = rope_2d(to_heads(k), rows, cols) * qk_scale
    v = to_heads(v)
    kc, vc = jnp.split(hc @ w_in[:, D_RET:3 * D_RET], 2, axis=-1)
    kc = to_heads(kc) * qk_scale
    vc = to_heads(vc)
    lg = -jnp.exp(log_decay.astype(jnp.float32))
    flip = lambda t: jnp.flip(t, axis=2)
    s_fwd = context_state(kc, vc, lg[0])
    s_bwd = context_state(flip(kc), flip(vc), lg[1])
    y = retention_chunkwise(q, k, v, lg[0], s_fwd) + flip(retention_chunkwise(flip(q), flip(k), flip(v), lg[1], s_bwd))
    y = head_group_norm(y).transpose(0, 2, 1, 3).reshape(b, seq_len, D_RET)
    ret = jax.nn.silu(g) * y.astype(g.dtype)
    sc = bg * short_conv(cg * xin, sc_w)
    return jnp.concatenate([ret, sc], axis=-1) @ w_out


def conformer_conv(h, w_pw1, b_pw1, dw_w, dw_b, ln_g, ln_b, w_pw2, b_pw2):
    a, gate = jnp.split(h @ w_pw1 + b_pw1, 2, axis=-1)
    u = a * jax.nn.sigmoid(gate)
    u = depthwise_conv(u, dw_w) + dw_b
    u = layer_norm(u, ln_g, ln_b)
    return jax.nn.silu(u) @ w_pw2 + b_pw2


def swiglu(h, w1, w3, w2):
    return (jax.nn.silu(h @ w1) * (h @ w3)) @ w2


def setup_inputs(seed: int = 0) -> dict:
    key = jax.random.key(seed)
    ks = jax.random.split(key, 24)
    d = D_MODEL
    n_even = (DEPTH + 1) // 2
    n_odd = DEPTH // 2
    nrm = lambda k, shape, s: jax.random.normal(k, shape, jnp.float32) * s
    base_decay = np.log(-np.log(1.0 - 2.0 ** (-5.0 - np.arange(N_RET_HEADS))))
    return {
        'x': nrm(ks[0], (BATCH, SEQ, d), 1.0),
        'c': nrm(ks[1], (BATCH, d), 1.0),
        'ctx': nrm(ks[2], (BATCH, CTX_LEN, d), 1.0),
        'c_ctx': nrm(ks[3], (d,), 1.0),
        'ada_w': nrm(ks[4], (DEPTH, d, N_MOD * d), 0.5 * d ** -0.5),
        'ada_b': nrm(ks[5], (DEPTH, N_MOD * d), 0.02),
        'norm_mix': 1.0 + nrm(ks[6], (DEPTH, d), 0.02),
        'norm_ffn': 1.0 + nrm(ks[7], (DEPTH, d), 0.02),
        'norm_final': 1.0 + nrm(ks[8], (d,), 0.02),
        'even_w_in': nrm(ks[9], (n_even, d, IN_COLS), d ** -0.5),
        'ret_log_decay': jnp.asarray(base_decay, jnp.float32)[None, None, :] + nrm(ks[10], (n_even, 2, N_RET_HEADS), 0.05),
        'sc_conv_w': nrm(ks[11], (n_even, SC_WIDTH, D_SC), SC_WIDTH ** -0.5),
        'even_w_out': nrm(ks[12], (n_even, d, d), d ** -0.5),
        'conf_w_pw1': nrm(ks[13], (n_odd, d, 2 * d), d ** -0.5),
        'conf_b_pw1': nrm(ks[14], (n_odd, 2 * d), 0.01),
        'conf_dw_w': nrm(ks[15], (n_odd, CONF_KERNEL, d), CONF_KERNEL ** -0.5),
        'conf_dw_b': nrm(ks[16], (n_odd, d), 0.01),
        'conf_ln_g': 1.0 + nrm(ks[17], (n_odd, d), 0.02),
        'conf_ln_b': nrm(ks[18], (n_odd, d), 0.02),
        'conf_w_pw2': nrm(ks[19], (n_odd, d, d), d ** -0.5),
        'conf_b_pw2': nrm(ks[20], (n_odd, d), 0.01),
        'ffn_w1': nrm(ks[21], (DEPTH, d, FFN_HIDDEN), d ** -0.5),
        'ffn_w3': nrm(ks[22], (DEPTH, d, FFN_HIDDEN), d ** -0.5),
        'ffn_w2': nrm(ks[23], (DEPTH, FFN_HIDDEN, d), FFN_HIDDEN ** -0.5),
    }


def reference(x, c, ctx, c_ctx, ada_w, ada_b, norm_mix, norm_ffn, norm_final, even_w_in, ret_log_decay, sc_conv_w, even_w_out, conf_w_pw1, conf_b_pw1, conf_dw_w, conf_dw_b, conf_ln_g, conf_ln_b, conf_w_pw2, conf_b_pw2, ffn_w1, ffn_w3, ffn_w2):
    b, seq_len, d = x.shape
    n_rows = seq_len // GRID_W
    rows = jnp.broadcast_to(jnp.arange(n_rows, dtype=jnp.float32)[:, None], (n_rows, GRID_W)).reshape(-1)
    cols = jnp.broadcast_to(jnp.arange(GRID_W, dtype=jnp.float32)[None, :], (n_rows, GRID_W)).reshape(-1)
    silu_c = jax.nn.silu(c)
    silu_cc = jax.nn.silu(c_ctx)
    for i in range(DEPTH):
        mod = silu_c @ ada_w[i] + ada_b[i]
        sh1, sc1, g1, sh2, sc2, g2 = [m[:, None, :] for m in jnp.split(mod, N_MOD, axis=-1)]
        h = modulate(rms_norm(x, norm_mix[i]), sh1, sc1)
        if i % 2 == 0:
            e = i // 2
            mod_c = silu_cc @ ada_w[i][:, :2 * d] + ada_b[i][:2 * d]
            hc = modulate(rms_norm(ctx, norm_mix[i]), mod_c[:d], mod_c[d:])
            mix = even_mixer(h, hc, even_w_in[e], ret_log_decay[e], sc_conv_w[e], even_w_out[e], rows, cols)
        else:
            o = i // 2
            mix = conformer_conv(h, conf_w_pw1[o], conf_b_pw1[o], conf_dw_w[o], conf_dw_b[o], conf_ln_g[o], conf_ln_b[o], conf_w_pw2[o], conf_b_pw2[o])
        x = x + g1 * mix
        h = modulate(rms_norm(x, norm_ffn[i]), sh2, sc2)
        x = x + g2 * swiglu(h, ffn_w1[i], ffn_w3[i], ffn_w2[i])
    return rms_norm(x, norm_final)
```

```python
import functools

import jax
import jax.numpy as jnp
from jax import lax
from jax.experimental import pallas as pl
from jax.experimental.pallas import tpu as pltpu

F32 = jnp.float32
BF16 = jnp.bfloat16

D_MODEL = 4096
SEQ = 16384
GRID_W = 64
CTX_LEN = 256
D_RET = D_MODEL // 2
HEAD_DIM = 256
N_HEADS = D_RET // HEAD_DIM
D_SC = D_MODEL - D_RET
SC_WIDTH = 3
ROPE_BASE = 10000.0
CONF_KERNEL = 31
FFN_HIDDEN = 11008
N_MOD = 6
EPS = 1e-6
QK_SCALE = HEAD_DIM ** -0.5

V7X_VMEM_BYTES = 64 * 1024 * 1024
VMEM_CEILING = V7X_VMEM_BYTES - 6 * 1024 * 1024
LANES = 128
FFN_PAD = 11264
RET_CHUNK = 256
RET_ROWS = 1024


def _params(sem, vmem_bytes):
    limit = min(int(vmem_bytes * 1.2) + (4 << 20), VMEM_CEILING)
    return pltpu.CompilerParams(dimension_semantics=sem, vmem_limit_bytes=limit)


def _silu(v):
    return v * jax.nn.sigmoid(v)


def _ada_kernel(cc_ref, w_ref, b_ref, o_ref):
    s = _silu(cc_ref[...]).astype(BF16)
    acc = jnp.dot(s, w_ref[...].astype(BF16), preferred_element_type=F32)
    o_ref[...] = acc + b_ref[...]


def _ada(cc8, ada_w, ada_b):
    depth, d, n = ada_w.shape
    tn = 512
    return pl.pallas_call(
        _ada_kernel,
        grid=(depth, n // tn),
        in_specs=[
            pl.BlockSpec((8, d), lambda l, j: (0, 0)),
            pl.BlockSpec((None, d, tn), lambda l, j: (l, 0, j)),
            pl.BlockSpec((None, 1, tn), lambda l, j: (l, 0, j)),
        ],
        out_specs=pl.BlockSpec((None, 8, tn), lambda l, j: (l, 0, j)),
        out_shape=jax.ShapeDtypeStruct((depth, 8, n), F32),
        compiler_params=_params(("arbitrary", "arbitrary"), 3 * d * tn * 4),
        name="ada_mod",
    )(cc8, ada_w, ada_b.reshape(depth, 1, n))


def _normmod_kernel(x_ref, g_ref, sh_ref, sc_ref, o_ref):
    x = x_ref[...]
    r = lax.rsqrt(jnp.mean(x * x, axis=-1, keepdims=True) + EPS)
    y = (x * r) * g_ref[...]
    o_ref[...] = (y * (1.0 + sc_ref[...]) + sh_ref[...]).astype(o_ref.dtype)


def _rmsnorm_kernel(x_ref, g_ref, o_ref):
    x = x_ref[...]
    r = lax.rsqrt(jnp.mean(x * x, axis=-1, keepdims=True) + EPS)
    o_ref[...] = ((x * r) * g_ref[...]).astype(o_ref.dtype)


def _normmod(x, g, shift, scale, tr=256):
    m, d = x.shape
    tr = min(tr, m)
    row = pl.BlockSpec((1, d), lambda i: (0, 0))
    return pl.pallas_call(
        _normmod_kernel,
        grid=(m // tr,),
        in_specs=[pl.BlockSpec((tr, d), lambda i: (i, 0)), row, row, row],
        out_specs=pl.BlockSpec((tr, d), lambda i: (i, 0)),
        out_shape=jax.ShapeDtypeStruct((m, d), BF16),
        compiler_params=_params(("parallel",), 6 * tr * d * 4),
        name="norm_modulate",
    )(x, g, shift, scale)


def _rmsnorm(x, g, tr=256):
    m, d = x.shape
    return pl.pallas_call(
        _rmsnorm_kernel,
        grid=(m // tr,),
        in_specs=[pl.BlockSpec((tr, d), lambda i: (i, 0)), pl.BlockSpec((1, d), lambda i: (0, 0))],
        out_specs=pl.BlockSpec((tr, d), lambda i: (i, 0)),
        out_shape=jax.ShapeDtypeStruct((m, d), F32),
        compiler_params=_params(("parallel",), 6 * tr * d * 4),
        name="final_norm",
    )(x, g)


def _mm_plain_kernel(a_ref, w_ref, o_ref):
    o_ref[...] = jnp.dot(a_ref[...], w_ref[...], preferred_element_type=F32).astype(o_ref.dtype)


def _mm_plain(a, w, col0, n, out_dtype, tm, tn):
    m, k = a.shape
    tm = min(tm, m)
    jb = col0 // tn
    vm = 2 * (tm * k * 2 + k * tn * 2 + tm * tn * 4) + tm * tn * 4
    return pl.pallas_call(
        _mm_plain_kernel,
        grid=(m // tm, n // tn),
        in_specs=[pl.BlockSpec((tm, k), lambda i, j: (i, 0)),
                  pl.BlockSpec((k, tn), lambda i, j: (0, j + jb))],
        out_specs=pl.BlockSpec((tm, tn), lambda i, j: (i, j)),
        out_shape=jax.ShapeDtypeStruct((m, n), out_dtype),
        compiler_params=_params(("parallel", "arbitrary"), vm),
        name="matmul_plain",
    )(a, w)


def _mm_rope_kernel(a_ref, w_ref, cos_ref, sin_ref, o_ref, *, n_rope_tiles, n_q_tiles, tn):
    acc = jnp.dot(a_ref[...], w_ref[...], preferred_element_type=F32)
    j = pl.program_id(1)

    @pl.when(j < n_rope_tiles)
    def _():
        cos = cos_ref[...]
        sin = sin_ref[...]
        scale = jnp.where(j >= n_q_tiles, QK_SCALE, 1.0).astype(F32)
        half = HEAD_DIM // 2
        for h in range(tn // HEAD_DIM):
            lo = h * HEAD_DIM
            t1 = acc[:, lo:lo + half]
            t2 = acc[:, lo + half:lo + HEAD_DIM]
            o_ref[:, lo:lo + half] = ((t1 * cos - t2 * sin) * scale).astype(o_ref.dtype)
            o_ref[:, lo + half:lo + HEAD_DIM] = ((t1 * sin + t2 * cos) * scale).astype(o_ref.dtype)

    @pl.when(j >= n_rope_tiles)
    def _():
        o_ref[...] = acc.astype(o_ref.dtype)


def _mm_rope(h, w, cos, sin, tm=1024, tn=1024):
    m, k = h.shape
    n = 3 * D_RET
    kern = functools.partial(_mm_rope_kernel, n_rope_tiles=2 * D_RET // tn, n_q_tiles=D_RET // tn, tn=tn)
    vm = 2 * (tm * k * 2 + k * tn * 2 + tm * tn * 2 + 2 * tm * LANES * 4) + 3 * tm * tn * 4
    return pl.pallas_call(
        kern,
        grid=(m // tm, n // tn),
        in_specs=[pl.BlockSpec((tm, k), lambda i, j: (i, 0)),
                  pl.BlockSpec((k, tn), lambda i, j: (0, j)),
                  pl.BlockSpec((tm, HEAD_DIM // 2), lambda i, j: (i, 0)),
                  pl.BlockSpec((tm, HEAD_DIM // 2), lambda i, j: (i, 0))],
        out_specs=pl.BlockSpec((tm, tn), lambda i, j: (i, j)),
        out_shape=jax.ShapeDtypeStruct((m, n), BF16),
        compiler_params=_params(("parallel", "arbitrary"), vm),
        name="matmul_qkv_rope",
    )(h, w, cos, sin)


def _mm_gated_kernel(a_ref, w1_ref, w2_ref, b1_ref, b2_ref, o_ref, *, act):
    a = a_ref[...]
    p = jnp.dot(a, w1_ref[...], preferred_element_type=F32) + b1_ref[...]
    q = jnp.dot(a, w2_ref[...], preferred_element_type=F32) + b2_ref[...]
    if act == "swiglu":
        o_ref[...] = (_silu(p) * q).astype(o_ref.dtype)
    else:
        o_ref[...] = (p * jax.nn.sigmoid(q)).astype(o_ref.dtype)


def _mm_gated(a, w1, w2, b1, b2, col2, n, act, out_dtype, tm=1024, tn=512):
    m, k = a.shape
    jb = col2 // tn
    osz = jnp.dtype(out_dtype).itemsize
    vm = 2 * (tm * k * 2 + 2 * k * tn * 2 + tm * tn * osz) + 3 * tm * tn * 4
    return pl.pallas_call(
        functools.partial(_mm_gated_kernel, act=act),
        grid=(m // tm, n // tn),
        in_specs=[pl.BlockSpec((tm, k), lambda i, j: (i, 0)),
                  pl.BlockSpec((k, tn), lambda i, j: (0, j)),
                  pl.BlockSpec((k, tn), lambda i, j: (0, j + jb)),
                  pl.BlockSpec((1, tn), lambda i, j: (0, j)),
                  pl.BlockSpec((1, tn), lambda i, j: (0, j + jb))],
        out_specs=pl.BlockSpec((tm, tn), lambda i, j: (i, j)),
        out_shape=jax.ShapeDtypeStruct((m, n), out_dtype),
        compiler_params=_params(("parallel", "arbitrary"), vm),
        name="matmul_gated_" + act,
    )(a, w1, w2, b1, b2)


def _mm_resid2_kernel(a1_ref, a2_ref, w_ref, b_ref, g_ref, x_ref, o_ref):
    k1 = a1_ref.shape[1]
    acc = jnp.dot(a1_ref[...], w_ref[0:k1, :], preferred_element_type=F32)
    acc = acc + jnp.dot(a2_ref[...], w_ref[k1:, :], preferred_element_type=F32)
    o_ref[...] = x_ref[...] + g_ref[...] * (acc + b_ref[...])


def _mm_resid2(a1, a2, w, bias, gate, xres, tm=1024, tn=512):
    m, k1 = a1.shape
    k2 = a2.shape[1]
    k, n = w.shape
    vm = 2 * (tm * k * 2 + k * tn * 2 + 2 * tm * tn * 4) + 2 * tm * tn * 4
    rowspec = pl.BlockSpec((1, tn), lambda i, j: (0, j))
    return pl.pallas_call(
        _mm_resid2_kernel,
        grid=(m // tm, n // tn),
        in_specs=[pl.BlockSpec((tm, k1), lambda i, j: (i, 0)),
                  pl.BlockSpec((tm, k2), lambda i, j: (i, 0)),
                  pl.BlockSpec((k, tn), lambda i, j: (0, j)),
                  rowspec, rowspec,
                  pl.BlockSpec((tm, tn), lambda i, j: (i, j))],
        out_specs=pl.BlockSpec((tm, tn), lambda i, j: (i, j)),
        out_shape=jax.ShapeDtypeStruct((m, n), F32),
        compiler_params=_params(("parallel", "arbitrary"), vm),
        name="matmul_resid2",
    )(a1, a2, w, bias, gate, xres)


def _mm_resid_kernel(a_ref, w_ref, b_ref, g_ref, x_ref, o_ref):
    acc = jnp.dot(a_ref[...], w_ref[...], preferred_element_type=F32)
    o_ref[...] = x_ref[...] + g_ref[...] * (acc + b_ref[...])


def _mm_resid(a, w, bias, gate, xres, tm=1024, tn=512):
    m, k = a.shape
    n = w.shape[1]
    vm = 2 * (tm * k * 2 + k * tn * 2 + 2 * tm * tn * 4) + 2 * tm * tn * 4
    rowspec = pl.BlockSpec((1, tn), lambda i, j: (0, j))
    return pl.pallas_call(
        _mm_resid_kernel,
        grid=(m // tm, n // tn),
        in_specs=[pl.BlockSpec((tm, k), lambda i, j: (i, 0)),
                  pl.BlockSpec((k, tn), lambda i, j: (0, j)),
                  rowspec, rowspec,
                  pl.BlockSpec((tm, tn), lambda i, j: (i, j))],
        out_specs=pl.BlockSpec((tm, tn), lambda i, j: (i, j)),
        out_shape=jax.ShapeDtypeStruct((m, n), F32),
        compiler_params=_params(("parallel", "arbitrary"), vm),
        name="matmul_resid",
    )(a, w, bias, gate, xres)


def _mm_resid_kgrid_kernel(a_ref, w_ref, g_ref, x_ref, o_ref, acc_ref):
    kk = pl.program_id(2)
    part = jnp.dot(a_ref[...], w_ref[...], preferred_element_type=F32)

    @pl.when(kk == 0)
    def _():
        acc_ref[...] = part

    @pl.when(kk > 0)
    def _():
        acc_ref[...] += part

    @pl.when(kk == pl.num_programs(2) - 1)
    def _():
        o_ref[...] = x_ref[...] + g_ref[...] * acc_ref[...]


def _mm_resid_kgrid(a, w, gate, xres, tm=1024, tn=1024, tk=2816):
    m, k = a.shape
    n = w.shape[1]
    vm = 2 * (tm * tk * 2 + tk * tn * 2 + 2 * tm * tn * 4) + 2 * tm * tn * 4
    return pl.pallas_call(
        _mm_resid_kgrid_kernel,
        grid=(m // tm, n // tn, k // tk),
        in_specs=[pl.BlockSpec((tm, tk), lambda i, j, kk: (i, kk)),
                  pl.BlockSpec((tk, tn), lambda i, j, kk: (kk, j)),
                  pl.BlockSpec((1, tn), lambda i, j, kk: (0, j)),
                  pl.BlockSpec((tm, tn), lambda i, j, kk: (i, j))],
        out_specs=pl.BlockSpec((tm, tn), lambda i, j, kk: (i, j)),
        out_shape=jax.ShapeDtypeStruct((m, n), F32),
        scratch_shapes=[pltpu.VMEM((tm, tn), F32)],
        compiler_params=_params(("parallel", "arbitrary", "arbitrary"), vm),
        name="matmul_resid_kgrid",
    )(a, w, gate, xres)


def _ret_kernel(*refs, backward):
    if backward:
        ld_ref, q_ref, k_ref, v_ref, kc_ref, vc_ref, of_ref, g_ref, o_ref, s_ref = refs
    else:
        ld_ref, q_ref, k_ref, v_ref, kc_ref, vc_ref, o_ref, s_ref = refs
    c = RET_CHUNK
    t = pl.program_id(1)
    lg = -jnp.exp(ld_ref[0:1, 0:1])
    ri = lax.broadcasted_iota(jnp.int32, (c, c), 0)
    ci = lax.broadcasted_iota(jnp.int32, (c, c), 1)
    dist = ((ci - ri) if backward else (ri - ci)).astype(F32)
    dmat = jnp.where(dist >= 0, jnp.exp(lg * jnp.maximum(dist, 0.0)), 0.0)
    pos = lax.broadcasted_iota(jnp.int32, (c, 1), 0).astype(F32)
    if backward:
        qdec = jnp.exp(lg * (c - pos))
        kdec = jnp.exp(lg * pos)
    else:
        qdec = jnp.exp(lg * (pos + 1.0))
        kdec = jnp.exp(lg * (c - 1.0 - pos))
    cdec = jnp.exp(lg * float(c))
    tdims = (((0,), (0,)), ((), ()))

    @pl.when(t == 0)
    def _():
        lc = kc_ref.shape[0]
        jpos = lax.broadcasted_iota(jnp.int32, (lc, 1), 0).astype(F32)
        w = jnp.exp(lg * (jpos if backward else (lc - 1.0 - jpos)))
        kcw = ((kc_ref[...] * QK_SCALE) * w).astype(BF16)
        s_ref[...] = lax.dot_general(kcw, vc_ref[...].astype(BF16), tdims, preferred_element_type=F32)

    n_chunks = q_ref.shape[0] // c
    order = range(n_chunks - 1, -1, -1) if backward else range(n_chunks)
    for cidx in order:
        rows = pl.ds(cidx * c, c)
        q = q_ref[rows, :]
        k = k_ref[rows, :]
        v = v_ref[rows, :]
        s = s_ref[...]
        scores = lax.dot_general(q, k, (((1,), (1,)), ((), ())), preferred_element_type=F32) * dmat
        intra = jnp.dot(scores.astype(BF16), v, preferred_element_type=F32)
        q_in = (q.astype(F32) * qdec).astype(BF16)
        cross = jnp.dot(q_in, s.astype(BF16), preferred_element_type=F32)
        k_in = (k.astype(F32) * kdec).astype(BF16)
        s_ref[...] = cdec * s + lax.dot_general(k_in, v, tdims, preferred_element_type=F32)
        o = intra + cross
        if backward:
            y = of_ref[rows, :] + o
            mu = jnp.mean(y, axis=-1, keepdims=True)
            yc = y - mu
            yn = yc * lax.rsqrt(jnp.mean(yc * yc, axis=-1, keepdims=True) + EPS)
            o_ref[rows, :] = (_silu(g_ref[rows, :]) * yn).astype(o_ref.dtype)
        else:
            o_ref[rows, :] = o


def _retention(ld, qkv, kvc, ofwd=None, rest=None):
    backward = ofwd is not None
    seq = qkv.shape[0]
    nt = seq // RET_ROWS
    tt = (lambda t: nt - 1 - t) if backward else (lambda t: t)
    dirn = 1 if backward else 0
    hd = HEAD_DIM
    blk = lambda off: pl.BlockSpec((RET_ROWS, hd), lambda h, t: (tt(t), h + off))
    in_specs = [
        pl.BlockSpec((None, None, 8, LANES), lambda h, t: (dirn, h, 0, 0)),
        blk(0), blk(N_HEADS), blk(2 * N_HEADS),
        pl.BlockSpec((CTX_LEN, hd), lambda h, t: (0, h)),
        pl.BlockSpec((CTX_LEN, hd), lambda h, t: (0, h + N_HEADS)),
    ]
    args = [ld, qkv, qkv, qkv, kvc, kvc]
    if backward:
        in_specs += [blk(0), blk(0)]
        args += [ofwd, rest]
    out_dtype = BF16 if backward else F32
    return pl.pallas_call(
        functools.partial(_ret_kernel, backward=backward),
        grid=(N_HEADS, nt),
        in_specs=in_specs,
        out_specs=blk(0),
        out_shape=jax.ShapeDtypeStruct((seq, D_RET), out_dtype),
        scratch_shapes=[pltpu.VMEM((hd, hd), F32)],
        compiler_params=_params(("arbitrary", "arbitrary"), 16 << 20),
        name="retention_bwd" if backward else "retention_fwd",
    )(*args)


def _sconv_kernel(bg_ref, cg_ref, xin_ref, cgp_ref, xinp_ref, cgn_ref, xinn_ref, w_ref, o_ref, p_ref):
    i = pl.program_id(0)
    tr = cg_ref.shape[0]
    p_ref[8:8 + tr, :] = cg_ref[...] * xin_ref[...]
    p_ref[0:8, :] = jnp.where(i > 0, cgp_ref[...] * xinp_ref[...], 0.0)
    p_ref[8 + tr:16 + tr, :] = jnp.where(i < pl.num_programs(0) - 1, cgn_ref[...] * xinn_ref[...], 0.0)
    conv = (w_ref[0:1, :] * p_ref[7:7 + tr, :] + w_ref[1:2, :] * p_ref[8:8 + tr, :]
            + w_ref[2:3, :] * p_ref[9:9 + tr, :])
    o_ref[...] = (bg_ref[...] * conv).astype(o_ref.dtype)


def _short_conv(rest, sc_w, tr=512, tc=1024):
    seq = rest.shape[0]
    nct = D_SC // tc
    r8 = tr // 8
    last8 = seq // 8 - 1
    main = lambda part: pl.BlockSpec((tr, tc), lambda i, j: (i, j + part * nct))
    prev = lambda part: pl.BlockSpec((8, tc), lambda i, j: (jnp.maximum(i * r8 - 1, 0), j + part * nct))
    nxt = lambda part: pl.BlockSpec((8, tc), lambda i, j: (jnp.minimum((i + 1) * r8, last8), j + part * nct))
    return pl.pallas_call(
        _sconv_kernel,
        grid=(seq // tr, nct),
        in_specs=[main(1), main(2), main(3), prev(2), prev(3), nxt(2), nxt(3),
                  pl.BlockSpec((SC_WIDTH, tc), lambda i, j: (0, j))],
        out_specs=pl.BlockSpec((tr, tc), lambda i, j: (i, j)),
        out_shape=jax.ShapeDtypeStruct((seq, D_SC), BF16),
        scratch_shapes=[pltpu.VMEM((tr + 16, tc), F32)],
        compiler_params=_params(("arbitrary", "arbitrary"), 10 * tr * tc * 4),
        name="short_conv",
    )(rest, rest, rest, rest, rest, rest, rest, sc_w)


CONF_HALO = 16
CONF_ROW_CHUNK = 32


def _dwconv_kernel(u_ref, up_ref, un_ref, w_ref, b_ref, g_ref, beta_ref, o_ref, p_ref, c_ref):
    i = pl.program_id(0)
    tr, d = u_ref.shape
    hl = CONF_HALO
    pad = (CONF_KERNEL - 1) // 2
    p_ref[hl:hl + tr, :] = u_ref[...]
    p_ref[0:hl, :] = jnp.where(i > 0, up_ref[...], 0.0)
    p_ref[hl + tr:2 * hl + tr, :] = jnp.where(i < pl.num_programs(0) - 1, un_ref[...], 0.0)
    rc = CONF_ROW_CHUNK

    def strip_body(s, carry):
        cols = pl.ds(pl.multiple_of(s * LANES, LANES), LANES)
        wv = w_ref[:, cols]
        bias = b_ref[:, cols]
        for r0 in range(0, tr, rc):
            base = r0 + hl - pad
            acc = wv[0:1, :] * p_ref[base:base + rc, cols]
            for j in range(1, CONF_KERNEL):
                acc = acc + wv[j:j + 1, :] * p_ref[base + j:base + j + rc, cols]
            c_ref[r0:r0 + rc, cols] = acc + bias
        return carry

    lax.fori_loop(0, d // LANES, strip_body, 0)
    x = c_ref[...]
    mu = jnp.mean(x, axis=-1, keepdims=True)
    xc = x - mu
    y = xc * lax.rsqrt(jnp.mean(xc * xc, axis=-1, keepdims=True) + EPS)
    y = y * g_ref[...] + beta_ref[...]
    o_ref[...] = _silu(y).astype(o_ref.dtype)


def _dwconv_ln_silu(u, dw_w, dw_b, ln_g, ln_b, tr=256):
    seq, d = u.shape
    hl = CONF_HALO
    rh = tr // hl
    lasth = seq // hl - 1
    row = pl.BlockSpec((1, d), lambda i: (0, 0))
    return pl.pallas_call(
        _dwconv_kernel,
        grid=(seq // tr,),
        in_specs=[pl.BlockSpec((tr, d), lambda i: (i, 0)),
                  pl.BlockSpec((hl, d), lambda i: (jnp.maximum(i * rh - 1, 0), 0)),
                  pl.BlockSpec((hl, d), lambda i: (jnp.minimum((i + 1) * rh, lasth), 0)),
                  pl.BlockSpec((CONF_KERNEL, d), lambda i: (0, 0)),
                  row, row, row],
        out_specs=pl.BlockSpec((tr, d), lambda i: (i, 0)),
        out_shape=jax.ShapeDtypeStruct((seq, d), BF16),
        scratch_shapes=[pltpu.VMEM((tr + 2 * hl, d), F32), pltpu.VMEM((tr, d), F32)],
        compiler_params=_params(("arbitrary",), 8 * tr * d * 4),
        name="dwconv_ln_silu",
    )(u, u, u, dw_w, dw_b, ln_g, ln_b)


def _rope_tables(seq):
    n_rows = seq // GRID_W
    rows = jnp.broadcast_to(jnp.arange(n_rows, dtype=F32)[:, None], (n_rows, GRID_W)).reshape(-1)
    cols = jnp.broadcast_to(jnp.arange(GRID_W, dtype=F32)[None, :], (n_rows, GRID_W)).reshape(-1)
    n_freq = HEAD_DIM // 4
    freqs = ROPE_BASE ** (-jnp.arange(n_freq, dtype=F32) / n_freq)
    ang = jnp.concatenate([rows[:, None] * freqs, cols[:, None] * freqs], axis=-1)
    return jnp.cos(ang), jnp.sin(ang)


def kernel(x, c, ctx, c_ctx, ada_w, ada_b, norm_mix, norm_ffn, norm_final, even_w_in, ret_log_decay, sc_conv_w, even_w_out, conf_w_pw1, conf_b_pw1, conf_dw_w, conf_dw_b, conf_ln_g, conf_ln_b, conf_w_pw2, conf_b_pw2, ffn_w1, ffn_w3, ffn_w2):
    d = D_MODEL
    seq = x.shape[1]
    xs = x[0]
    row = lambda v: v.reshape(1, -1)
    zeros_d = jnp.zeros((1, d), F32)

    cc8 = jnp.zeros((8, d), F32).at[0].set(c[0]).at[1].set(c_ctx)
    mods = _ada(cc8, ada_w, ada_b)
    cos, sin = _rope_tables(seq)
    hpad = FFN_PAD - FFN_HIDDEN

    for i in range(2):
        m = mods[i, 0]
        sh1, sc1, g1, sh2, sc2, g2 = [row(m[n * d:(n + 1) * d]) for n in range(N_MOD)]
        h = _normmod(xs, row(norm_mix[i]), sh1, sc1)
        if i == 0:
            w_in = even_w_in[0].astype(BF16)
            w_out = even_w_out[0].astype(BF16)
            mc = mods[0, 1]
            hc = _normmod(ctx[0], row(norm_mix[0]), row(mc[:d]), row(mc[d:2 * d]))
            qkv = _mm_rope(h, w_in, cos, sin)
            rest = _mm_plain(h, w_in, 3 * D_RET, D_RET + 3 * D_SC, F32, 1024, 1024)
            kvc = _mm_plain(hc, w_in, D_RET, 2 * D_RET, F32, CTX_LEN, 1024)
            ld = jnp.broadcast_to(ret_log_decay[0][:, :, None, None], (2, N_HEADS, 8, LANES))
            ofwd = _retention(ld, qkv, kvc)
            ret = _retention(ld, qkv, kvc, ofwd, rest)
            sc = _short_conv(rest, sc_conv_w[0])
            xs = _mm_resid2(ret, sc, w_out, zeros_d, g1, xs)
        else:
            w_pw1 = conf_w_pw1[0].astype(BF16)
            w_pw2 = conf_w_pw2[0].astype(BF16)
            b_pw1 = row(conf_b_pw1[0])
            u = _mm_gated(h, w_pw1, w_pw1, b_pw1, b_pw1, d, d, "glu", F32)
            v = _dwconv_ln_silu(u, conf_dw_w[0], row(conf_dw_b[0]), row(conf_ln_g[0]), row(conf_ln_b[0]))
            xs = _mm_resid(v, w_pw2, row(conf_b_pw2[0]), g1, xs)
        h = _normmod(xs, row(norm_ffn[i]), sh2, sc2)
        w1 = jnp.pad(ffn_w1[i].astype(BF16), ((0, 0), (0, hpad)))
        w3 = jnp.pad(ffn_w3[i].astype(BF16), ((0, 0), (0, hpad)))
        w2 = jnp.pad(ffn_w2[i].astype(BF16), ((0, hpad), (0, 0)))
        zb = jnp.zeros((1, FFN_PAD), F32)
        u = _mm_gated(h, w1, w3, zb, zb, 0, FFN_PAD, "swiglu", BF16)
        xs = _mm_resid_kgrid(u, w2, g2, xs)
    out = _rmsnorm(xs, row(norm_final))
    return out[None]
```

```python
import functools

import jax
import jax.numpy as jnp
from jax import lax
from jax.experimental import pallas as pl
from jax.experimental.pallas import tpu as pltpu

F32 = jnp.float32
BF16 = jnp.bfloat16

D_MODEL = 4096
SEQ = 16384
GRID_W = 64
CTX_LEN = 256
D_RET = D_MODEL // 2
HEAD_DIM = 256
N_HEADS = D_RET // HEAD_DIM
D_SC = D_MODEL - D_RET
SC_WIDTH = 3
ROPE_BASE = 10000.0
CONF_KERNEL = 31
FFN_HIDDEN = 11008
N_MOD = 6
EPS = 1e-6
QK_SCALE = HEAD_DIM ** -0.5

V7X_VMEM_BYTES = 64 * 1024 * 1024
VMEM_CEILING = V7X_VMEM_BYTES - 6 * 1024 * 1024
LANES = 128
RET_CHUNK = 256
RET_ROWS = 1024


def _params(sem, vmem_bytes):
    limit = min(int(vmem_bytes * 1.2) + (4 << 20), VMEM_CEILING)
    return pltpu.CompilerParams(dimension_semantics=sem, vmem_limit_bytes=limit)


def _silu(v):
    return v * jax.nn.sigmoid(v)


def _ada_kernel(cc_ref, w_ref, b_ref, o_ref):
    s = _silu(cc_ref[...]).astype(BF16)
    acc = jnp.dot(s, w_ref[...].astype(BF16), preferred_element_type=F32)
    o_ref[...] = acc + b_ref[...]


def _ada(cc8, ada_w, ada_b):
    depth, d, n = ada_w.shape
    tn = 512
    return pl.pallas_call(
        _ada_kernel,
        grid=(depth, n // tn),
        in_specs=[
            pl.BlockSpec((8, d), lambda l, j: (0, 0)),
            pl.BlockSpec((None, d, tn), lambda l, j: (l, 0, j)),
            pl.BlockSpec((None, 1, tn), lambda l, j: (l, 0, j)),
        ],
        out_specs=pl.BlockSpec((None, 8, tn), lambda l, j: (l, 0, j)),
        out_shape=jax.ShapeDtypeStruct((depth, 8, n), F32),
        compiler_params=_params(("arbitrary", "arbitrary"), 3 * d * tn * 4),
        name="ada_mod",
    )(cc8, ada_w, ada_b.reshape(depth, 1, n))


def _normmod_kernel(x_ref, g_ref, sh_ref, sc_ref, o_ref):
    x = x_ref[...]
    r = lax.rsqrt(jnp.mean(x * x, axis=-1, keepdims=True) + EPS)
    y = (x * r) * g_ref[...]
    o_ref[...] = (y * (1.0 + sc_ref[...]) + sh_ref[...]).astype(o_ref.dtype)


def _rmsnorm_kernel(x_ref, g_ref, o_ref):
    x = x_ref[...]
    r = lax.rsqrt(jnp.mean(x * x, axis=-1, keepdims=True) + EPS)
    o_ref[...] = ((x * r) * g_ref[...]).astype(o_ref.dtype)


def _normmod(x, g, shift, scale, tr=256):
    m, d = x.shape
    tr = min(tr, m)
    row = pl.BlockSpec((1, d), lambda i: (0, 0))
    return pl.pallas_call(
        _normmod_kernel,
        grid=(m // tr,),
        in_specs=[pl.BlockSpec((tr, d), lambda i: (i, 0)), row, row, row],
        out_specs=pl.BlockSpec((tr, d), lambda i: (i, 0)),
        out_shape=jax.ShapeDtypeStruct((m, d), BF16),
        compiler_params=_params(("parallel",), 6 * tr * d * 4),
        name="norm_modulate",
    )(x, g, shift, scale)


def _rmsnorm(x, g, tr=256):
    m, d = x.shape
    return pl.pallas_call(
        _rmsnorm_kernel,
        grid=(m // tr,),
        in_specs=[pl.BlockSpec((tr, d), lambda i: (i, 0)), pl.BlockSpec((1, d), lambda i: (0, 0))],
        out_specs=pl.BlockSpec((tr, d), lambda i: (i, 0)),
        out_shape=jax.ShapeDtypeStruct((m, d), F32),
        compiler_params=_params(("parallel",), 6 * tr * d * 4),
        name="final_norm",
    )(x, g)


def _mm_plain_kernel(a_ref, w_ref, o_ref):
    o_ref[...] = jnp.dot(a_ref[...], w_ref[...], preferred_element_type=F32).astype(o_ref.dtype)


def _mm_plain(a, w, col0, n, out_dtype, tm, tn):
    m, k = a.shape
    tm = min(tm, m)
    jb = col0 // tn
    vm = 2 * (tm * k * 2 + k * tn * 2 + tm * tn * 4) + tm * tn * 4
    return pl.pallas_call(
        _mm_plain_kernel,
        grid=(m // tm, n // tn),
        in_specs=[pl.BlockSpec((tm, k), lambda i, j: (i, 0)),
                  pl.BlockSpec((k, tn), lambda i, j: (0, j + jb))],
        out_specs=pl.BlockSpec((tm, tn), lambda i, j: (i, j)),
        out_shape=jax.ShapeDtypeStruct((m, n), out_dtype),
        compiler_params=_params(("parallel", "arbitrary"), vm),
        name="matmul_plain",
    )(a, w)


def _mm_rope_kernel(a_ref, w_ref, cos_ref, sin_ref, o_ref, *, n_rope_tiles, n_q_tiles, tn):
    acc = jnp.dot(a_ref[...], w_ref[...], preferred_element_type=F32)
    j = pl.program_id(1)

    @pl.when(j < n_rope_tiles)
    def _():
        cos = cos_ref[...]
        sin = sin_ref[...]
        scale = jnp.where(j >= n_q_tiles, QK_SCALE, 1.0).astype(F32)
        half = HEAD_DIM // 2
        for h in range(tn // HEAD_DIM):
            lo = h * HEAD_DIM
            t1 = acc[:, lo:lo + half]
            t2 = acc[:, lo + half:lo + HEAD_DIM]
            o_ref[:, lo:lo + half] = ((t1 * cos - t2 * sin) * scale).astype(o_ref.dtype)
            o_ref[:, lo + half:lo + HEAD_DIM] = ((t1 * sin + t2 * cos) * scale).astype(o_ref.dtype)

    @pl.when(j >= n_rope_tiles)
    def _():
        o_ref[...] = acc.astype(o_ref.dtype)


def _mm_rope(h, w, cos, sin, tm=1024, tn=1024):
    m, k = h.shape
    n = 3 * D_RET
    kern = functools.partial(_mm_rope_kernel, n_rope_tiles=2 * D_RET // tn, n_q_tiles=D_RET // tn, tn=tn)
    vm = 2 * (tm * k * 2 + k * tn * 2 + tm * tn * 2 + 2 * tm * LANES * 4) + 3 * tm * tn * 4
    return pl.pallas_call(
        kern,
        grid=(m // tm, n // tn),
        in_specs=[pl.BlockSpec((tm, k), lambda i, j: (i, 0)),
                  pl.BlockSpec((k, tn), lambda i, j: (0, j)),
                  pl.BlockSpec((tm, HEAD_DIM // 2), lambda i, j: (i, 0)),
                  pl.BlockSpec((tm, HEAD_DIM // 2), lambda i, j: (i, 0))],
        out_specs=pl.BlockSpec((tm, tn), lambda i, j: (i, j)),
        out_shape=jax.ShapeDtypeStruct((m, n), BF16),
        compiler_params=_params(("parallel", "arbitrary"), vm),
        name="matmul_qkv_rope",
    )(h, w, cos, sin)


def _mm_gated_kernel(a_ref, w1_ref, w2_ref, b1_ref, b2_ref, o_ref, *, act):
    a = a_ref[...]
    p = jnp.dot(a, w1_ref[...], preferred_element_type=F32) + b1_ref[...]
    q = jnp.dot(a, w2_ref[...], preferred_element_type=F32) + b2_ref[...]
    if act == "swiglu":
        o_ref[...] = (_silu(p) * q).astype(o_ref.dtype)
    else:
        o_ref[...] = (p * jax.nn.sigmoid(q)).astype(o_ref.dtype)


def _mm_gated(a, w1, w2, b1, b2, col2, n, act, out_dtype, tm=1024, tn=512):
    m, k = a.shape
    jb = col2 // tn
    osz = jnp.dtype(out_dtype).itemsize
    vm = 2 * (tm * k * 2 + 2 * k * tn * 2 + tm * tn * osz) + 3 * tm * tn * 4
    return pl.pallas_call(
        functools.partial(_mm_gated_kernel, act=act),
        grid=(m // tm, n // tn),
        in_specs=[pl.BlockSpec((tm, k), lambda i, j: (i, 0)),
                  pl.BlockSpec((k, tn), lambda i, j: (0, j)),
                  pl.BlockSpec((k, tn), lambda i, j: (0, j + jb)),
                  pl.BlockSpec((1, tn), lambda i, j: (0, j)),
                  pl.BlockSpec((1, tn), lambda i, j: (0, j + jb))],
        out_specs=pl.BlockSpec((tm, tn), lambda i, j: (i, j)),
        out_shape=jax.ShapeDtypeStruct((m, n), out_dtype),
        compiler_params=_params(("parallel", "arbitrary"), vm),
        name="matmul_gated_" + act,
    )(a, w1, w2, b1, b2)


def _mm_swiglu_kernel(a_ref, w1_ref, w3_ref, o_ref):
    a = a_ref[...]
    p = jnp.dot(a, w1_ref[...], preferred_element_type=F32)
    q = jnp.dot(a, w3_ref[...], preferred_element_type=F32)
    o_ref[...] = (_silu(p) * q).astype(o_ref.dtype)


def _mm_swiglu(a, w1, w3, layer, tm=2048, tn=256):
    m, k = a.shape
    n = w1.shape[2]
    vm = 2 * (tm * k * 2 + 2 * k * tn * 2 + tm * tn * 2) + 3 * tm * tn * 4
    wspec = pl.BlockSpec((None, k, tn), lambda i, j: (layer, 0, j))
    return pl.pallas_call(
        _mm_swiglu_kernel,
        grid=(m // tm, n // tn),
        in_specs=[pl.BlockSpec((tm, k), lambda i, j: (i, 0)), wspec, wspec],
        out_specs=pl.BlockSpec((tm, tn), lambda i, j: (i, j)),
        out_shape=jax.ShapeDtypeStruct((m, n), BF16),
        compiler_params=_params(("parallel", "arbitrary"), vm),
        name="matmul_swiglu",
    )(a, w1, w3)


def _mm_resid_wres_kernel(a_ref, w_ref, g_ref, x_ref, o_ref):
    acc = jnp.dot(a_ref[...], w_ref[...], preferred_element_type=F32)
    o_ref[...] = x_ref[...] + g_ref[...] * acc


def _mm_resid_wres(a, w, layer, gate, xres, tm=256, tn=1024):
    m, k = a.shape
    n = w.shape[2]
    vm = 2 * (tm * k * 2 + 2 * tm * tn * 4) + k * tn * 2 + 2 * tm * tn * 4
    return pl.pallas_call(
        _mm_resid_wres_kernel,
        grid=(n // tn, m // tm),
        in_specs=[pl.BlockSpec((tm, k), lambda j, i: (i, 0)),
                  pl.BlockSpec((None, k, tn), lambda j, i: (layer, 0, j), pipeline_mode=pl.Buffered(1)),
                  pl.BlockSpec((1, tn), lambda j, i: (0, j)),
                  pl.BlockSpec((tm, tn), lambda j, i: (i, j))],
        out_specs=pl.BlockSpec((tm, tn), lambda j, i: (i, j)),
        out_shape=jax.ShapeDtypeStruct((m, n), F32),
        compiler_params=_params(("arbitrary", "arbitrary"), vm),
        name="matmul_resid_wres",
    )(a, w, gate, xres)


def _mm_resid2_kernel(a1_ref, a2_ref, w_ref, b_ref, g_ref, x_ref, o_ref):
    k1 = a1_ref.shape[1]
    acc = jnp.dot(a1_ref[...], w_ref[0:k1, :], preferred_element_type=F32)
    acc = acc + jnp.dot(a2_ref[...], w_ref[k1:, :], preferred_element_type=F32)
    o_ref[...] = x_ref[...] + g_ref[...] * (acc + b_ref[...])


def _mm_resid2(a1, a2, w, bias, gate, xres, tm=1024, tn=512):
    m, k1 = a1.shape
    k2 = a2.shape[1]
    k, n = w.shape
    vm = 2 * (tm * k * 2 + k * tn * 2 + 2 * tm * tn * 4) + 2 * tm * tn * 4
    rowspec = pl.BlockSpec((1, tn), lambda i, j: (0, j))
    return pl.pallas_call(
        _mm_resid2_kernel,
        grid=(m // tm, n // tn),
        in_specs=[pl.BlockSpec((tm, k1), lambda i, j: (i, 0)),
                  pl.BlockSpec((tm, k2), lambda i, j: (i, 0)),
                  pl.BlockSpec((k, tn), lambda i, j: (0, j)),
                  rowspec, rowspec,
                  pl.BlockSpec((tm, tn), lambda i, j: (i, j))],
        out_specs=pl.BlockSpec((tm, tn), lambda i, j: (i, j)),
        out_shape=jax.ShapeDtypeStruct((m, n), F32),
        compiler_params=_params(("parallel", "arbitrary"), vm),
        name="matmul_resid2",
    )(a1, a2, w, bias, gate, xres)


def _mm_resid_kernel(a_ref, w_ref, b_ref, g_ref, x_ref, o_ref):
    acc = jnp.dot(a_ref[...], w_ref[...], preferred_element_type=F32)
    o_ref[...] = x_ref[...] + g_ref[...] * (acc + b_ref[...])


def _mm_resid(a, w, bias, gate, xres, tm=1024, tn=512):
    m, k = a.shape
    n = w.shape[1]
    vm = 2 * (tm * k * 2 + k * tn * 2 + 2 * tm * tn * 4) + 2 * tm * tn * 4
    rowspec = pl.BlockSpec((1, tn), lambda i, j: (0, j))
    return pl.pallas_call(
        _mm_resid_kernel,
        grid=(m // tm, n // tn),
        in_specs=[pl.BlockSpec((tm, k), lambda i, j: (i, 0)),
                  pl.BlockSpec((k, tn), lambda i, j: (0, j)),
                  rowspec, rowspec,
                  pl.BlockSpec((tm, tn), lambda i, j: (i, j))],
        out_specs=pl.BlockSpec((tm, tn), lambda i, j: (i, j)),
        out_shape=jax.ShapeDtypeStruct((m, n), F32),
        compiler_params=_params(("parallel", "arbitrary"), vm),
        name="matmul_resid",
    )(a, w, bias, gate, xres)


def _ret_kernel(*refs, backward):
    if backward:
        ld_ref, q_ref, k_ref, v_ref, kc_ref, vc_ref, of_ref, g_ref, o_ref, s_ref = refs
    else:
        ld_ref, q_ref, k_ref, v_ref, kc_ref, vc_ref, o_ref, s_ref = refs
    c = RET_CHUNK
    t = pl.program_id(1)
    lg = -jnp.exp(ld_ref[0:1, 0:1])
    ri = lax.broadcasted_iota(jnp.int32, (c, c), 0)
    ci = lax.broadcasted_iota(jnp.int32, (c, c), 1)
    dist = ((ci - ri) if backward else (ri - ci)).astype(F32)
    dmat = jnp.where(dist >= 0, jnp.exp(lg * jnp.maximum(dist, 0.0)), 0.0)
    pos = lax.broadcasted_iota(jnp.int32, (c, 1), 0).astype(F32)
    if backward:
        qdec = jnp.exp(lg * (c - pos))
        kdec = jnp.exp(lg * pos)
    else:
        qdec = jnp.exp(lg * (pos + 1.0))
        kdec = jnp.exp(lg * (c - 1.0 - pos))
    cdec = jnp.exp(lg * float(c))
    tdims = (((0,), (0,)), ((), ()))

    @pl.when(t == 0)
    def _():
        lc = kc_ref.shape[0]
        jpos = lax.broadcasted_iota(jnp.int32, (lc, 1), 0).astype(F32)
        w = jnp.exp(lg * (jpos if backward else (lc - 1.0 - jpos)))
        kcw = ((kc_ref[...] * QK_SCALE) * w).astype(BF16)
        s_ref[...] = lax.dot_general(kcw, vc_ref[...].astype(BF16), tdims, preferred_element_type=F32)

    n_chunks = q_ref.shape[0] // c
    order = range(n_chunks - 1, -1, -1) if backward else range(n_chunks)
    for cidx in order:
        rows = pl.ds(cidx * c, c)
        q = q_ref[rows, :]
        k = k_ref[rows, :]
        v = v_ref[rows, :]
        s = s_ref[...]
        scores = lax.dot_general(q, k, (((1,), (1,)), ((), ())), preferred_element_type=F32) * dmat
        intra = jnp.dot(scores.astype(BF16), v, preferred_element_type=F32)
        q_in = (q.astype(F32) * qdec).astype(BF16)
        cross = jnp.dot(q_in, s.astype(BF16), preferred_element_type=F32)
        k_in = (k.astype(F32) * kdec).astype(BF16)
        s_ref[...] = cdec * s + lax.dot_general(k_in, v, tdims, preferred_element_type=F32)
        o = intra + cross
        if backward:
            y = of_ref[rows, :] + o
            mu = jnp.mean(y, axis=-1, keepdims=True)
            yc = y - mu
            yn = yc * lax.rsqrt(jnp.mean(yc * yc, axis=-1, keepdims=True) + EPS)
            o_ref[rows, :] = (_silu(g_ref[rows, :]) * yn).astype(o_ref.dtype)
        else:
            o_ref[rows, :] = o


def _retention(ld, qkv, kvc, ofwd=None, rest=None):
    backward = ofwd is not None
    seq = qkv.shape[0]
    nt = seq // RET_ROWS
    tt = (lambda t: nt - 1 - t) if backward else (lambda t: t)
    dirn = 1 if backward else 0
    hd = HEAD_DIM
    blk = lambda off: pl.BlockSpec((RET_ROWS, hd), lambda h, t: (tt(t), h + off))
    in_specs = [
        pl.BlockSpec((None, None, 8, LANES), lambda h, t: (dirn, h, 0, 0)),
        blk(0), blk(N_HEADS), blk(2 * N_HEADS),
        pl.BlockSpec((CTX_LEN, hd), lambda h, t: (0, h)),
        pl.BlockSpec((CTX_LEN, hd), lambda h, t: (0, h + N_HEADS)),
    ]
    args = [ld, qkv, qkv, qkv, kvc, kvc]
    if backward:
        in_specs += [blk(0), blk(0)]
        args += [ofwd, rest]
    out_dtype = BF16 if backward else F32
    return pl.pallas_call(
        functools.partial(_ret_kernel, backward=backward),
        grid=(N_HEADS, nt),
        in_specs=in_specs,
        out_specs=blk(0),
        out_shape=jax.ShapeDtypeStruct((seq, D_RET), out_dtype),
        scratch_shapes=[pltpu.VMEM((hd, hd), F32)],
        compiler_params=_params(("arbitrary", "arbitrary"), 16 << 20),
        name="retention_bwd" if backward else "retention_fwd",
    )(*args)


def _sconv_kernel(bg_ref, cg_ref, xin_ref, cgp_ref, xinp_ref, cgn_ref, xinn_ref, w_ref, o_ref, p_ref):
    i = pl.program_id(0)
    tr = cg_ref.shape[0]
    p_ref[8:8 + tr, :] = cg_ref[...] * xin_ref[...]
    p_ref[0:8, :] = jnp.where(i > 0, cgp_ref[...] * xinp_ref[...], 0.0)
    p_ref[8 + tr:16 + tr, :] = jnp.where(i < pl.num_programs(0) - 1, cgn_ref[...] * xinn_ref[...], 0.0)
    conv = (w_ref[0:1, :] * p_ref[7:7 + tr, :] + w_ref[1:2, :] * p_ref[8:8 + tr, :]
            + w_ref[2:3, :] * p_ref[9:9 + tr, :])
    o_ref[...] = (bg_ref[...] * conv).astype(o_ref.dtype)


def _short_conv(rest, sc_w, tr=512, tc=1024):
    seq = rest.shape[0]
    nct = D_SC // tc
    r8 = tr // 8
    last8 = seq // 8 - 1
    main = lambda part: pl.BlockSpec((tr, tc), lambda i, j: (i, j + part * nct))
    prev = lambda part: pl.BlockSpec((8, tc), lambda i, j: (jnp.maximum(i * r8 - 1, 0), j + part * nct))
    nxt = lambda part: pl.BlockSpec((8, tc), lambda i, j: (jnp.minimum((i + 1) * r8, last8), j + part * nct))
    return pl.pallas_call(
        _sconv_kernel,
        grid=(seq // tr, nct),
        in_specs=[main(1), main(2), main(3), prev(2), prev(3), nxt(2), nxt(3),
                  pl.BlockSpec((SC_WIDTH, tc), lambda i, j: (0, j))],
        out_specs=pl.BlockSpec((tr, tc), lambda i, j: (i, j)),
        out_shape=jax.ShapeDtypeStruct((seq, D_SC), BF16),
        scratch_shapes=[pltpu.VMEM((tr + 16, tc), F32)],
        compiler_params=_params(("arbitrary", "arbitrary"), 10 * tr * tc * 4),
        name="short_conv",
    )(rest, rest, rest, rest, rest, rest, rest, sc_w)


CONF_HALO = 16
CONF_ROW_CHUNK = 64


def _dwconv_kernel(u_ref, up_ref, un_ref, w_ref, b_ref, g_ref, beta_ref, o_ref, p_ref, c_ref):
    i = pl.program_id(0)
    tr, d = u_ref.shape
    hl = CONF_HALO
    pad = (CONF_KERNEL - 1) // 2
    p_ref[hl:hl + tr, :] = u_ref[...]
    p_ref[0:hl, :] = jnp.where(i > 0, up_ref[...], 0.0)
    p_ref[hl + tr:2 * hl + tr, :] = jnp.where(i < pl.num_programs(0) - 1, un_ref[...], 0.0)
    rc = CONF_ROW_CHUNK

    off = hl - pad

    def strip_body(s, carry):
        cols = pl.ds(pl.multiple_of(s * LANES, LANES), LANES)
        wv = w_ref[:, cols]
        bias = b_ref[:, cols]
        for r0 in range(0, tr, rc):
            out = None
            for e in range(8):
                q = None
                for j in range(CONF_KERNEL):
                    if (off + j) % 8 != e:
                        continue
                    a8 = r0 + (off + j) - e
                    term = wv[j:j + 1, :] * p_ref[a8:a8 + rc + 8, cols]
                    q = term if q is None else q + term
                part = q[e:e + rc, :]
                out = part if out is None else out + part
            c_ref[r0:r0 + rc, cols] = out + bias
        return carry

    lax.fori_loop(0, d // LANES, strip_body, 0)
    x = c_ref[...]
    mu = jnp.mean(x, axis=-1, keepdims=True)
    xc = x - mu
    y = xc * lax.rsqrt(jnp.mean(xc * xc, axis=-1, keepdims=True) + EPS)
    y = y * g_ref[...] + beta_ref[...]
    o_ref[...] = _silu(y).astype(o_ref.dtype)


def _dwconv_ln_silu(u, dw_w, dw_b, ln_g, ln_b, tr=256):
    seq, d = u.shape
    hl = CONF_HALO
    rh = tr // hl
    lasth = seq // hl - 1
    row = pl.BlockSpec((1, d), lambda i: (0, 0))
    return pl.pallas_call(
        _dwconv_kernel,
        grid=(seq // tr,),
        in_specs=[pl.BlockSpec((tr, d), lambda i: (i, 0)),
                  pl.BlockSpec((hl, d), lambda i: (jnp.maximum(i * rh - 1, 0), 0)),
                  pl.BlockSpec((hl, d), lambda i: (jnp.minimum((i + 1) * rh, lasth), 0)),
                  pl.BlockSpec((CONF_KERNEL, d), lambda i: (0, 0)),
                  row, row, row],
        out_specs=pl.BlockSpec((tr, d), lambda i: (i, 0)),
        out_shape=jax.ShapeDtypeStruct((seq, d), BF16),
        scratch_shapes=[pltpu.VMEM((tr + 2 * hl, d), F32), pltpu.VMEM((tr, d), F32)],
        compiler_params=_params(("arbitrary",), 8 * tr * d * 4),
        name="dwconv_ln_silu",
    )(u, u, u, dw_w, dw_b, ln_g, ln_b)


def _rope_tables(seq):
    n_rows = seq // GRID_W
    rows = jnp.broadcast_to(jnp.arange(n_rows, dtype=F32)[:, None], (n_rows, GRID_W)).reshape(-1)
    cols = jnp.broadcast_to(jnp.arange(GRID_W, dtype=F32)[None, :], (n_rows, GRID_W)).reshape(-1)
    n_freq = HEAD_DIM // 4
    freqs = ROPE_BASE ** (-jnp.arange(n_freq, dtype=F32) / n_freq)
    ang = jnp.concatenate([rows[:, None] * freqs, cols[:, None] * freqs], axis=-1)
    return jnp.cos(ang), jnp.sin(ang)


def kernel(x, c, ctx, c_ctx, ada_w, ada_b, norm_mix, norm_ffn, norm_final, even_w_in, ret_log_decay, sc_conv_w, even_w_out, conf_w_pw1, conf_b_pw1, conf_dw_w, conf_dw_b, conf_ln_g, conf_ln_b, conf_w_pw2, conf_b_pw2, ffn_w1, ffn_w3, ffn_w2):
    d = D_MODEL
    seq = x.shape[1]
    xs = x[0]
    row = lambda v: v.reshape(1, -1)
    zeros_d = jnp.zeros((1, d), F32)

    cc8 = jnp.zeros((8, d), F32).at[0].set(c[0]).at[1].set(c_ctx)
    mods = _ada(cc8, ada_w, ada_b)
    cos, sin = _rope_tables(seq)
    w1, w3, w2 = ffn_w1.astype(BF16), ffn_w3.astype(BF16), ffn_w2.astype(BF16)

    for i in range(2):
        m = mods[i, 0]
        sh1, sc1, g1, sh2, sc2, g2 = [row(m[n * d:(n + 1) * d]) for n in range(N_MOD)]
        h = _normmod(xs, row(norm_mix[i]), sh1, sc1)
        if i == 0:
            w_in = even_w_in[0].astype(BF16)
            w_out = even_w_out[0].astype(BF16)
            mc = mods[0, 1]
            hc = _normmod(ctx[0], row(norm_mix[0]), row(mc[:d]), row(mc[d:2 * d]))
            qkv = _mm_rope(h, w_in, cos, sin)
            rest = _mm_plain(h, w_in, 3 * D_RET, D_RET + 3 * D_SC, F32, 1024, 1024)
            kvc = _mm_plain(hc, w_in, D_RET, 2 * D_RET, F32, CTX_LEN, 1024)
            ld = jnp.broadcast_to(ret_log_decay[0][:, :, None, None], (2, N_HEADS, 8, LANES))
            ofwd = _retention(ld, qkv, kvc)
            ret = _retention(ld, qkv, kvc, ofwd, rest)
            sc = _short_conv(rest, sc_conv_w[0])
            xs = _mm_resid2(ret, sc, w_out, zeros_d, g1, xs)
        else:
            w_pw1 = conf_w_pw1[0].astype(BF16)
            w_pw2 = conf_w_pw2[0].astype(BF16)
            b_pw1 = row(conf_b_pw1[0])
            u = _mm_gated(h, w_pw1, w_pw1, b_pw1, b_pw1, d, d, "glu", F32)
            v = _dwconv_ln_silu(u, conf_dw_w[0], row(conf_dw_b[0]), row(conf_ln_g[0]), row(conf_ln_b[0]))
            xs = _mm_resid(v, w_pw2, row(conf_b_pw2[0]), g1, xs)
        h = _normmod(xs, row(norm_ffn[i]), sh2, sc2)
        u = _mm_swiglu(h, w1, w3, i)
        xs = _mm_resid_wres(u, w2, i, g2, xs)
    out = _rmsnorm(xs, row(norm_final))
    return out[None]
```

```python
import functools

import jax
import jax.numpy as jnp
from jax import lax
from jax.experimental import pallas as pl
from jax.experimental.pallas import tpu as pltpu

F32 = jnp.float32
BF16 = jnp.bfloat16

D_MODEL = 4096
SEQ = 16384
GRID_W = 64
CTX_LEN = 256
D_RET = D_MODEL // 2
HEAD_DIM = 256
N_HEADS = D_RET // HEAD_DIM
D_SC = D_MODEL - D_RET
SC_WIDTH = 3
ROPE_BASE = 10000.0
CONF_KERNEL = 31
FFN_HIDDEN = 11008
N_MOD = 6
EPS = 1e-6
QK_SCALE = HEAD_DIM ** -0.5

V7X_VMEM_BYTES = 64 * 1024 * 1024
VMEM_CEILING = V7X_VMEM_BYTES - 6 * 1024 * 1024
LANES = 128
RET_CHUNK = 256
RET_ROWS = 1024
RET_HEADS = 4


def _params(sem, vmem_bytes):
    limit = min(int(vmem_bytes * 1.2) + (4 << 20), VMEM_CEILING)
    return pltpu.CompilerParams(dimension_semantics=sem, vmem_limit_bytes=limit)


def _silu(v):
    return v * jax.nn.sigmoid(v)


def _ada_kernel(cc_ref, w_ref, b_ref, o_ref):
    s = _silu(cc_ref[...]).astype(BF16)
    acc = jnp.dot(s, w_ref[...].astype(BF16), preferred_element_type=F32)
    o_ref[...] = acc + b_ref[...]


def _ada(cc8, ada_w, ada_b):
    depth, d, n = ada_w.shape
    tn = 512
    return pl.pallas_call(
        _ada_kernel,
        grid=(depth, n // tn),
        in_specs=[
            pl.BlockSpec((8, d), lambda l, j: (0, 0)),
            pl.BlockSpec((None, d, tn), lambda l, j: (l, 0, j)),
            pl.BlockSpec((None, 1, tn), lambda l, j: (l, 0, j)),
        ],
        out_specs=pl.BlockSpec((None, 8, tn), lambda l, j: (l, 0, j)),
        out_shape=jax.ShapeDtypeStruct((depth, 8, n), F32),
        compiler_params=_params(("arbitrary", "arbitrary"), 3 * d * tn * 4),
        name="ada_mod",
    )(cc8, ada_w, ada_b.reshape(depth, 1, n))


def _normmod_kernel(x_ref, g_ref, sh_ref, sc_ref, o_ref):
    x = x_ref[...]
    r = lax.rsqrt(jnp.mean(x * x, axis=-1, keepdims=True) + EPS)
    y = (x * r) * g_ref[...]
    o_ref[...] = (y * (1.0 + sc_ref[...]) + sh_ref[...]).astype(o_ref.dtype)


def _rmsnorm_kernel(x_ref, g_ref, o_ref):
    x = x_ref[...]
    r = lax.rsqrt(jnp.mean(x * x, axis=-1, keepdims=True) + EPS)
    o_ref[...] = ((x * r) * g_ref[...]).astype(o_ref.dtype)


def _normmod(x, g, shift, scale, tr=256):
    m, d = x.shape
    tr = min(tr, m)
    row = pl.BlockSpec((1, d), lambda i: (0, 0))
    return pl.pallas_call(
        _normmod_kernel,
        grid=(m // tr,),
        in_specs=[pl.BlockSpec((tr, d), lambda i: (i, 0)), row, row, row],
        out_specs=pl.BlockSpec((tr, d), lambda i: (i, 0)),
        out_shape=jax.ShapeDtypeStruct((m, d), BF16),
        compiler_params=_params(("parallel",), 6 * tr * d * 4),
        name="norm_modulate",
    )(x, g, shift, scale)


def _rmsnorm(x, g, tr=256):
    m, d = x.shape
    return pl.pallas_call(
        _rmsnorm_kernel,
        grid=(m // tr,),
        in_specs=[pl.BlockSpec((tr, d), lambda i: (i, 0)), pl.BlockSpec((1, d), lambda i: (0, 0))],
        out_specs=pl.BlockSpec((tr, d), lambda i: (i, 0)),
        out_shape=jax.ShapeDtypeStruct((m, d), F32),
        compiler_params=_params(("parallel",), 6 * tr * d * 4),
        name="final_norm",
    )(x, g)


MXU_N = 256


def _dot(a, b):
    return jnp.dot(a, b, preferred_element_type=F32)


def _call(body, grid, in_specs, args, out_spec, out_shape, vmem, name, cast_jobs=()):
    steps = 1
    for g in grid:
        steps *= g
    job_in, job_out, job_shapes, job_args = [], [], [], []
    for src, layer in cast_jobs:
        r, c = src.shape[-2:]
        br = next(b for b in (16, 32, 64, 128, 256, 512, 1024, 2048, 4096) if r % b == 0 and r // b <= steps)
        nblk = r // br

        def blk(*g, nblk=nblk):
            s = g[0]
            for gi, n in zip(g[1:], grid[1:]):
                s = s * n + gi
            return jnp.minimum(s, nblk - 1)

        if layer is None:
            job_in.append(pl.BlockSpec((br, c), lambda *g, blk=blk: (blk(*g), 0)))
        else:
            job_in.append(pl.BlockSpec((None, br, c), lambda *g, blk=blk, layer=layer: (layer, blk(*g), 0)))
        job_out.append(pl.BlockSpec((br, c), lambda *g, blk=blk: (blk(*g), 0)))
        job_shapes.append(jax.ShapeDtypeStruct((r, c), BF16))
        job_args.append(src)
        vmem += 2 * br * c * (4 + 2)
    n_in, n_jobs = len(in_specs), len(cast_jobs)

    def kern(*refs):
        ins = refs[:n_in]
        jin = refs[n_in:n_in + n_jobs]
        out = refs[n_in + n_jobs]
        jout = refs[n_in + n_jobs + 1:]
        for s_ref, d_ref in zip(jin, jout):
            d_ref[...] = s_ref[...].astype(d_ref.dtype)
        body(*ins, out)

    res = pl.pallas_call(
        kern,
        grid=grid,
        in_specs=list(in_specs) + job_in,
        out_specs=[out_spec] + job_out,
        out_shape=[out_shape] + job_shapes,
        compiler_params=_params(("arbitrary",) * len(grid), vmem),
        name=name,
    )(*args, *job_args)
    return res if cast_jobs else res[0]


def _mm_plain_body(a_ref, w_ref, o_ref):
    a = a_ref[...]
    for c0 in range(0, o_ref.shape[1], MXU_N):
        o_ref[:, c0:c0 + MXU_N] = _dot(a, w_ref[:, c0:c0 + MXU_N]).astype(o_ref.dtype)


def _mm_plain(a, w, col0, n, out_dtype, tm, tn, cast_jobs=()):
    m, k = a.shape
    tm = min(tm, m)
    jb = col0 // tn
    osz = jnp.dtype(out_dtype).itemsize
    vm = 2 * (tm * k * 2 + k * tn * 2 + tm * tn * osz) + 2 * tm * MXU_N * 4
    return _call(
        _mm_plain_body, (m // tm, n // tn),
        [pl.BlockSpec((tm, k), lambda i, j: (i, 0)), pl.BlockSpec((k, tn), lambda i, j: (0, j + jb))],
        (a, w), pl.BlockSpec((tm, tn), lambda i, j: (i, j)), jax.ShapeDtypeStruct((m, n), out_dtype),
        vm, "matmul_plain", cast_jobs)


def _mm_rope_body(a_ref, w_ref, cos_ref, sin_ref, o_ref, *, n_rope_tiles, n_q_tiles):
    j = pl.program_id(1)
    tn = o_ref.shape[1]
    half = HEAD_DIM // 2

    @pl.when(j < n_rope_tiles)
    def _():
        a = a_ref[...]
        cos = cos_ref[...]
        sin = sin_ref[...]
        scale = jnp.where(j >= n_q_tiles, QK_SCALE, 1.0).astype(F32)
        for lo in range(0, tn, HEAD_DIM):
            acc = _dot(a, w_ref[:, lo:lo + HEAD_DIM])
            t1 = acc[:, :half]
            t2 = acc[:, half:]
            o_ref[:, lo:lo + half] = ((t1 * cos - t2 * sin) * scale).astype(o_ref.dtype)
            o_ref[:, lo + half:lo + HEAD_DIM] = ((t1 * sin + t2 * cos) * scale).astype(o_ref.dtype)

    @pl.when(j >= n_rope_tiles)
    def _():
        _mm_plain_body(a_ref, w_ref, o_ref)


def _mm_rope(h, w, cos, sin, tm=1024, tn=1024, cast_jobs=()):
    m, k = h.shape
    n = 3 * D_RET
    body = functools.partial(_mm_rope_body, n_rope_tiles=2 * D_RET // tn, n_q_tiles=D_RET // tn)
    vm = 2 * (tm * k * 2 + k * tn * 2 + tm * tn * 2 + 2 * tm * LANES * 4) + 4 * tm * MXU_N * 4
    tab = pl.BlockSpec((tm, HEAD_DIM // 2), lambda i, j: (i, 0))
    return _call(
        body, (m // tm, n // tn),
        [pl.BlockSpec((tm, k), lambda i, j: (i, 0)), pl.BlockSpec((k, tn), lambda i, j: (0, j)), tab, tab],
        (h, w, cos, sin), pl.BlockSpec((tm, tn), lambda i, j: (i, j)), jax.ShapeDtypeStruct((m, n), BF16),
        vm, "matmul_qkv_rope", cast_jobs)


def _mm_glu_body(a_ref, wa_ref, wg_ref, ba_ref, bg_ref, o_ref):
    a = a_ref[...]
    for c0 in range(0, o_ref.shape[1], MXU_N):
        cs = slice(c0, c0 + MXU_N)
        p = _dot(a, wa_ref[:, cs]) + ba_ref[:, cs]
        q = _dot(a, wg_ref[:, cs]) + bg_ref[:, cs]
        o_ref[:, cs] = (p * jax.nn.sigmoid(q)).astype(o_ref.dtype)


def _mm_glu(a, w, b, n, out_dtype, tm=1024, tn=512, cast_jobs=()):
    m, k = a.shape
    jb = n // tn
    osz = jnp.dtype(out_dtype).itemsize
    vm = 2 * (tm * k * 2 + 2 * k * tn * 2 + tm * tn * osz) + 4 * tm * MXU_N * 4
    return _call(
        _mm_glu_body, (m // tm, n // tn),
        [pl.BlockSpec((tm, k), lambda i, j: (i, 0)),
         pl.BlockSpec((k, tn), lambda i, j: (0, j)),
         pl.BlockSpec((k, tn), lambda i, j: (0, j + jb)),
         pl.BlockSpec((1, tn), lambda i, j: (0, j)),
         pl.BlockSpec((1, tn), lambda i, j: (0, j + jb))],
        (a, w, w, b, b), pl.BlockSpec((tm, tn), lambda i, j: (i, j)), jax.ShapeDtypeStruct((m, n), out_dtype),
        vm, "matmul_glu", cast_jobs)


SWIGLU_ROWS = 1024


def _mm_swiglu_body(a_ref, w1_ref, w3_ref, o_ref):
    for r0 in range(0, a_ref.shape[0], SWIGLU_ROWS):
        rs = slice(r0, r0 + SWIGLU_ROWS)
        a = a_ref[rs, :]
        p = _dot(a, w1_ref[...])
        q = _dot(a, w3_ref[...])
        o_ref[rs, :] = (_silu(p) * q).astype(o_ref.dtype)


def _mm_swiglu(a, w1, w3, tm=2048, tn=256, cast_jobs=()):
    m, k = a.shape
    n = w1.shape[1]
    vm = 2 * (tm * k * 2 + 2 * k * tn * 2 + tm * tn * 2) + 4 * SWIGLU_ROWS * tn * 4
    wspec = pl.BlockSpec((k, tn), lambda i, j: (0, j))
    return _call(
        _mm_swiglu_body, (m // tm, n // tn),
        [pl.BlockSpec((tm, k), lambda i, j: (i, 0)), wspec, wspec],
        (a, w1, w3), pl.BlockSpec((tm, tn), lambda i, j: (i, j)), jax.ShapeDtypeStruct((m, n), BF16),
        vm, "matmul_swiglu", cast_jobs)


def _mm_resid_body(*refs, n_a, has_bias):
    a_refs = refs[:n_a]
    w_ref = refs[n_a]
    b_ref = refs[n_a + 1] if has_bias else None
    g_ref, x_ref, o_ref = refs[n_a + has_bias + 1:]
    a_vals = [r[...] for r in a_refs]
    for c0 in range(0, o_ref.shape[1], MXU_N):
        cs = slice(c0, c0 + MXU_N)
        k0 = 0
        acc = None
        for a in a_vals:
            part = _dot(a, w_ref[k0:k0 + a.shape[1], cs])
            acc = part if acc is None else acc + part
            k0 += a.shape[1]
        if has_bias:
            acc = acc + b_ref[:, cs]
        o_ref[:, cs] = x_ref[:, cs] + g_ref[:, cs] * acc


def _mm_resid(a_list, w, gate, xres, bias=None, tm=1024, tn=512, w_resident=False, cast_jobs=()):
    m = a_list[0].shape[0]
    k, n = w.shape
    vm = 2 * (tm * k * 2 + 2 * tm * tn * 4) + (1 if w_resident else 2) * k * tn * 2 + 2 * tm * MXU_N * 4
    if w_resident:
        grid = (n // tn, m // tm)
        ij = lambda f: (lambda j, i: f(i, j))
        wspec = pl.BlockSpec((k, tn), ij(lambda i, j: (0, j)), pipeline_mode=pl.Buffered(1))
    else:
        grid = (m // tm, n // tn)
        ij = lambda f: f
        wspec = pl.BlockSpec((k, tn), lambda i, j: (0, j))
    rowspec = pl.BlockSpec((1, tn), ij(lambda i, j: (0, j)))
    tile = pl.BlockSpec((tm, tn), ij(lambda i, j: (i, j)))
    in_specs = [pl.BlockSpec((tm, a.shape[1]), ij(lambda i, j: (i, 0))) for a in a_list] + [wspec]
    args = list(a_list) + [w]
    if bias is not None:
        in_specs.append(rowspec)
        args.append(bias)
    in_specs += [rowspec, tile]
    args += [gate, xres]
    body = functools.partial(_mm_resid_body, n_a=len(a_list), has_bias=bias is not None)
    return _call(body, grid, in_specs, args, tile, jax.ShapeDtypeStruct((m, n), F32), vm,
                 "matmul_resid_wres" if w_resident else "matmul_resid", cast_jobs)


def _ret_kernel(*refs, backward):
    if backward:
        ld_ref, q_ref, k_ref, v_ref, kc_ref, vc_ref, of_ref, g_ref, o_ref, s_ref = refs
    else:
        ld_ref, q_ref, k_ref, v_ref, kc_ref, vc_ref, o_ref, s_ref = refs
    c = RET_CHUNK
    hd = HEAD_DIM
    t = pl.program_id(1)
    ri = lax.broadcasted_iota(jnp.int32, (c, c), 0)
    ci = lax.broadcasted_iota(jnp.int32, (c, c), 1)
    dist = ((ci - ri) if backward else (ri - ci)).astype(F32)
    pos = lax.broadcasted_iota(jnp.int32, (c, 1), 0).astype(F32)
    tdims = (((0,), (0,)), ((), ()))
    n_chunks = q_ref.shape[0] // c
    order = range(n_chunks - 1, -1, -1) if backward else range(n_chunks)

    heads = []
    for hh in range(RET_HEADS):
        lg = -jnp.exp(ld_ref[hh, 0:1, 0:1])
        dmat = jnp.where(dist >= 0, jnp.exp(lg * jnp.maximum(dist, 0.0)), 0.0)
        if backward:
            qdec = jnp.exp(lg * (c - pos))
            kdec = jnp.exp(lg * pos)
        else:
            qdec = jnp.exp(lg * (pos + 1.0))
            kdec = jnp.exp(lg * (c - 1.0 - pos))
        cdec = jnp.exp(lg * float(c))
        heads.append((lg, dmat, qdec, kdec, cdec))

    @pl.when(t == 0)
    def _():
        lc = kc_ref.shape[0]
        jpos = lax.broadcasted_iota(jnp.int32, (lc, 1), 0).astype(F32)
        for hh in range(RET_HEADS):
            cs = slice(hh * hd, (hh + 1) * hd)
            w = jnp.exp(heads[hh][0] * (jpos if backward else (lc - 1.0 - jpos)))
            kcw = ((kc_ref[:, cs] * QK_SCALE) * w).astype(BF16)
            s_ref[hh] = lax.dot_general(kcw, vc_ref[:, cs].astype(BF16), tdims, preferred_element_type=F32)

    for cidx in order:
        rows = pl.ds(cidx * c, c)
        for hh in range(RET_HEADS):
            _, dmat, qdec, kdec, cdec = heads[hh]
            cs = slice(hh * hd, (hh + 1) * hd)
            q = q_ref[rows, cs]
            k = k_ref[rows, cs]
            v = v_ref[rows, cs]
            s = s_ref[hh]
            scores = lax.dot_general(q, k, (((1,), (1,)), ((), ())), preferred_element_type=F32) * dmat
            intra = jnp.dot(scores.astype(BF16), v, preferred_element_type=F32)
            q_in = (q.astype(F32) * qdec).astype(BF16)
            cross = jnp.dot(q_in, s.astype(BF16), preferred_element_type=F32)
            k_in = (k.astype(F32) * kdec).astype(BF16)
            s_ref[hh] = cdec * s + lax.dot_general(k_in, v, tdims, preferred_element_type=F32)
            o = intra + cross
            if backward:
                y = of_ref[rows, cs] + o
                mu = jnp.mean(y, axis=-1, keepdims=True)
                yc = y - mu
                yn = yc * lax.rsqrt(jnp.mean(yc * yc, axis=-1, keepdims=True) + EPS)
                o_ref[rows, cs] = (_silu(g_ref[rows, cs]) * yn).astype(o_ref.dtype)
            else:
                o_ref[rows, cs] = o


def _retention(ld, qkv, kvc, ofwd=None, rest=None):
    backward = ofwd is not None
    seq = qkv.shape[0]
    nt = seq // RET_ROWS
    tt = (lambda t: nt - 1 - t) if backward else (lambda t: t)
    dirn = 1 if backward else 0
    hd = HEAD_DIM
    wd = RET_HEADS * hd
    ng = N_HEADS // RET_HEADS
    blk = lambda off: pl.BlockSpec((RET_ROWS, wd), lambda h, t: (tt(t), h + off))
    in_specs = [
        pl.BlockSpec((None, RET_HEADS, 8, LANES), lambda h, t: (dirn, h, 0, 0)),
        blk(0), blk(ng), blk(2 * ng),
        pl.BlockSpec((CTX_LEN, wd), lambda h, t: (0, h)),
        pl.BlockSpec((CTX_LEN, wd), lambda h, t: (0, h + ng)),
    ]
    args = [ld, qkv, qkv, qkv, kvc, kvc]
    if backward:
        in_specs += [blk(0), blk(0)]
        args += [ofwd, rest]
    out_dtype = BF16 if backward else F32
    tile = RET_ROWS * wd
    vm = 2 * (3 * tile * 2 + 2 * CTX_LEN * wd * 4 + (2 * tile * 4 if backward else 0)
              + tile * jnp.dtype(out_dtype).itemsize) + 8 * RET_HEADS * RET_CHUNK * RET_CHUNK * 4
    return pl.pallas_call(
        functools.partial(_ret_kernel, backward=backward),
        grid=(ng, nt),
        in_specs=in_specs,
        out_specs=blk(0),
        out_shape=jax.ShapeDtypeStruct((seq, D_RET), out_dtype),
        scratch_shapes=[pltpu.VMEM((RET_HEADS, hd, hd), F32)],
        compiler_params=_params(("arbitrary", "arbitrary"), vm),
        name="retention_bwd" if backward else "retention_fwd",
    )(*args)


def _sconv_kernel(bg_ref, cg_ref, xin_ref, cgp_ref, xinp_ref, cgn_ref, xinn_ref, w_ref, o_ref, p_ref):
    i = pl.program_id(0)
    tr = cg_ref.shape[0]
    p_ref[8:8 + tr, :] = cg_ref[...] * xin_ref[...]
    p_ref[0:8, :] = jnp.where(i > 0, cgp_ref[...] * xinp_ref[...], 0.0)
    p_ref[8 + tr:16 + tr, :] = jnp.where(i < pl.num_programs(0) - 1, cgn_ref[...] * xinn_ref[...], 0.0)
    conv = (w_ref[0:1, :] * p_ref[7:7 + tr, :] + w_ref[1:2, :] * p_ref[8:8 + tr, :]
            + w_ref[2:3, :] * p_ref[9:9 + tr, :])
    o_ref[...] = (bg_ref[...] * conv).astype(o_ref.dtype)


def _short_conv(rest, sc_w, tr=512, tc=1024):
    seq = rest.shape[0]
    nct = D_SC // tc
    r8 = tr // 8
    last8 = seq // 8 - 1
    main = lambda part: pl.BlockSpec((tr, tc), lambda i, j: (i, j + part * nct))
    prev = lambda part: pl.BlockSpec((8, tc), lambda i, j: (jnp.maximum(i * r8 - 1, 0), j + part * nct))
    nxt = lambda part: pl.BlockSpec((8, tc), lambda i, j: (jnp.minimum((i + 1) * r8, last8), j + part * nct))
    return pl.pallas_call(
        _sconv_kernel,
        grid=(seq // tr, nct),
        in_specs=[main(1), main(2), main(3), prev(2), prev(3), nxt(2), nxt(3),
                  pl.BlockSpec((SC_WIDTH, tc), lambda i, j: (0, j))],
        out_specs=pl.BlockSpec((tr, tc), lambda i, j: (i, j)),
        out_shape=jax.ShapeDtypeStruct((seq, D_SC), BF16),
        scratch_shapes=[pltpu.VMEM((tr + 16, tc), F32)],
        compiler_params=_params(("arbitrary", "arbitrary"), 10 * tr * tc * 4),
        name="short_conv",
    )(rest, rest, rest, rest, rest, rest, rest, sc_w)


CONF_HALO = 16
CONF_ROW_CHUNK = 64


def _dwconv_kernel(u_ref, up_ref, un_ref, w_ref, b_ref, g_ref, beta_ref, o_ref, p_ref, c_ref):
    i = pl.program_id(0)
    tr, d = u_ref.shape
    hl = CONF_HALO
    pad = (CONF_KERNEL - 1) // 2
    p_ref[hl:hl + tr, :] = u_ref[...]
    p_ref[0:hl, :] = jnp.where(i > 0, up_ref[...], 0.0)
    p_ref[hl + tr:2 * hl + tr, :] = jnp.where(i < pl.num_programs(0) - 1, un_ref[...], 0.0)
    rc = CONF_ROW_CHUNK

    off = hl - pad

    def strip_body(s, carry):
        cols = pl.ds(pl.multiple_of(s * LANES, LANES), LANES)
        wv = w_ref[:, cols]
        bias = b_ref[:, cols]
        for r0 in range(0, tr, rc):
            out = None
            for e in range(8):
                q = None
                for j in range(CONF_KERNEL):
                    if (off + j) % 8 != e:
                        continue
                    a8 = r0 + (off + j) - e
                    term = wv[j:j + 1, :] * p_ref[a8:a8 + rc + 8, cols]
                    q = term if q is None else q + term
                part = q[e:e + rc, :]
                out = part if out is None else out + part
            c_ref[r0:r0 + rc, cols] = out + bias
        return carry

    lax.fori_loop(0, d // LANES, strip_body, 0)
    x = c_ref[...]
    mu = jnp.mean(x, axis=-1, keepdims=True)
    xc = x - mu
    y = xc * lax.rsqrt(jnp.mean(xc * xc, axis=-1, keepdims=True) + EPS)
    y = y * g_ref[...] + beta_ref[...]
    o_ref[...] = _silu(y).astype(o_ref.dtype)


def _dwconv_ln_silu(u, dw_w, dw_b, ln_g, ln_b, tr=256):
    seq, d = u.shape
    hl = CONF_HALO
    rh = tr // hl
    lasth = seq // hl - 1
    row = pl.BlockSpec((1, d), lambda i: (0, 0))
    return pl.pallas_call(
        _dwconv_kernel,
        grid=(seq // tr,),
        in_specs=[pl.BlockSpec((tr, d), lambda i: (i, 0)),
                  pl.BlockSpec((hl, d), lambda i: (jnp.maximum(i * rh - 1, 0), 0)),
                  pl.BlockSpec((hl, d), lambda i: (jnp.minimum((i + 1) * rh, lasth), 0)),
                  pl.BlockSpec((CONF_KERNEL, d), lambda i: (0, 0)),
                  row, row, row],
        out_specs=pl.BlockSpec((tr, d), lambda i: (i, 0)),
        out_shape=jax.ShapeDtypeStruct((seq, d), BF16),
        scratch_shapes=[pltpu.VMEM((tr + 2 * hl, d), F32), pltpu.VMEM((tr, d), F32)],
        compiler_params=_params(("arbitrary",), 8 * tr * d * 4),
        name="dwconv_ln_silu",
    )(u, u, u, dw_w, dw_b, ln_g, ln_b)


def _rope_tables(seq):
    n_rows = seq // GRID_W
    rows = jnp.broadcast_to(jnp.arange(n_rows, dtype=F32)[:, None], (n_rows, GRID_W)).reshape(-1)
    cols = jnp.broadcast_to(jnp.arange(GRID_W, dtype=F32)[None, :], (n_rows, GRID_W)).reshape(-1)
    n_freq = HEAD_DIM // 4
    freqs = ROPE_BASE ** (-jnp.arange(n_freq, dtype=F32) / n_freq)
    ang = jnp.concatenate([rows[:, None] * freqs, cols[:, None] * freqs], axis=-1)
    return jnp.cos(ang), jnp.sin(ang)


def kernel(x, c, ctx, c_ctx, ada_w, ada_b, norm_mix, norm_ffn, norm_final, even_w_in, ret_log_decay, sc_conv_w, even_w_out, conf_w_pw1, conf_b_pw1, conf_dw_w, conf_dw_b, conf_ln_g, conf_ln_b, conf_w_pw2, conf_b_pw2, ffn_w1, ffn_w3, ffn_w2):
    d = D_MODEL
    seq = x.shape[1]
    xs = x[0]
    row = lambda v: v.reshape(1, -1)

    cc8 = jnp.zeros((8, d), F32).at[0].set(c[0]).at[1].set(c_ctx)
    mods = _ada(cc8, ada_w, ada_b)
    cos, sin = _rope_tables(seq)
    mod_rows = lambda i: [row(mods[i, 0][n * d:(n + 1) * d]) for n in range(N_MOD)]

    sh1, sc1, g1, sh2, sc2, g2 = mod_rows(0)
    h = _normmod(xs, row(norm_mix[0]), sh1, sc1)
    mc = mods[0, 1]
    hc = _normmod(ctx[0], row(norm_mix[0]), row(mc[:d]), row(mc[d:2 * d]))
    w_in = even_w_in[0].astype(BF16)
    qkv, w_out, w1_0 = _mm_rope(h, w_in, cos, sin, cast_jobs=((even_w_out, 0), (ffn_w1, 0)))
    rest, w3_0 = _mm_plain(h, w_in, 3 * D_RET, D_RET + 3 * D_SC, F32, 1024, 1024, cast_jobs=((ffn_w3, 0),))
    kvc = _mm_plain(hc, w_in, D_RET, 2 * D_RET, F32, CTX_LEN, 1024)
    ld = jnp.broadcast_to(ret_log_decay[0][:, :, None, None], (2, N_HEADS, 8, LANES))
    ofwd = _retention(ld, qkv, kvc)
    ret = _retention(ld, qkv, kvc, ofwd, rest)
    sc = _short_conv(rest, sc_conv_w[0])
    xs, w2_0 = _mm_resid([ret, sc], w_out, g1, xs, cast_jobs=((ffn_w2, 0),))
    h = _normmod(xs, row(norm_ffn[0]), sh2, sc2)
    u, w1_1, w3_1 = _mm_swiglu(h, w1_0, w3_0, cast_jobs=((ffn_w1, 1), (ffn_w3, 1)))
    xs, w2_1, w_pw1, w_pw2 = _mm_resid(
        [u], w2_0, g2, xs, tm=256, tn=1024, w_resident=True,
        cast_jobs=((ffn_w2, 1), (conf_w_pw1, 0), (conf_w_pw2, 0)))

    sh1, sc1, g1, sh2, sc2, g2 = mod_rows(1)
    h = _normmod(xs, row(norm_mix[1]), sh1, sc1)
    u = _mm_glu(h, w_pw1, row(conf_b_pw1[0]), d, F32)
    v = _dwconv_ln_silu(u, conf_dw_w[0], row(conf_dw_b[0]), row(conf_ln_g[0]), row(conf_ln_b[0]))
    xs = _mm_resid([v], w_pw2, g1, xs, bias=row(conf_b_pw2[0]))
    h = _normmod(xs, row(norm_ffn[1]), sh2, sc2)
    u = _mm_swiglu(h, w1_1, w3_1)
    xs = _mm_resid([u], w2_1, g2, xs, tm=256, tn=1024, w_resident=True)
    out = _rmsnorm(xs, row(norm_final))
    return out[None]
```

```python
import functools

import jax
import jax.numpy as jnp
from jax import lax
from jax.experimental import pallas as pl
from jax.experimental.pallas import tpu as pltpu

F32 = jnp.float32
BF16 = jnp.bfloat16

D_MODEL = 4096
SEQ = 16384
GRID_W = 64
CTX_LEN = 256
D_RET = D_MODEL // 2
HEAD_DIM = 256
N_HEADS = D_RET // HEAD_DIM
D_SC = D_MODEL - D_RET
SC_WIDTH = 3
ROPE_BASE = 10000.0
CONF_KERNEL = 31
FFN_HIDDEN = 11008
N_MOD = 6
EPS = 1e-6
QK_SCALE = HEAD_DIM ** -0.5

V7X_VMEM_BYTES = 64 * 1024 * 1024
VMEM_CEILING = V7X_VMEM_BYTES - 6 * 1024 * 1024
LANES = 128
RET_CHUNK = 256
RET_ROWS = 1024
RET_HEADS = 4


def _params(sem, vmem_bytes):
    limit = min(int(vmem_bytes * 1.2) + (4 << 20), VMEM_CEILING)
    return pltpu.CompilerParams(dimension_semantics=sem, vmem_limit_bytes=limit)


def _silu(v):
    return v * jax.nn.sigmoid(v)


def _ada_kernel(cc_ref, w_ref, b_ref, o_ref):
    s = _silu(cc_ref[...]).astype(BF16)
    acc = jnp.dot(s, w_ref[...].astype(BF16), preferred_element_type=F32)
    o_ref[...] = acc + b_ref[...]


def _ada(cc8, ada_w, ada_b):
    depth, d, n = ada_w.shape
    tn = 512
    return pl.pallas_call(
        _ada_kernel,
        grid=(depth, n // tn),
        in_specs=[
            pl.BlockSpec((8, d), lambda l, j: (0, 0)),
            pl.BlockSpec((None, d, tn), lambda l, j: (l, 0, j)),
            pl.BlockSpec((None, 1, tn), lambda l, j: (l, 0, j)),
        ],
        out_specs=pl.BlockSpec((None, 8, tn), lambda l, j: (l, 0, j)),
        out_shape=jax.ShapeDtypeStruct((depth, 8, n), F32),
        compiler_params=_params(("arbitrary", "arbitrary"), 3 * d * tn * 4),
        name="ada_mod",
    )(cc8, ada_w, ada_b.reshape(depth, 1, n))


def _normmod_kernel(x_ref, g_ref, sh_ref, sc_ref, o_ref):
    x = x_ref[...]
    r = lax.rsqrt(jnp.mean(x * x, axis=-1, keepdims=True) + EPS)
    y = (x * r) * g_ref[...]
    o_ref[...] = (y * (1.0 + sc_ref[...]) + sh_ref[...]).astype(o_ref.dtype)


def _rmsnorm_kernel(x_ref, g_ref, o_ref):
    x = x_ref[...]
    r = lax.rsqrt(jnp.mean(x * x, axis=-1, keepdims=True) + EPS)
    o_ref[...] = ((x * r) * g_ref[...]).astype(o_ref.dtype)


def _normmod(x, g, shift, scale, tr=256):
    m, d = x.shape
    tr = min(tr, m)
    row = pl.BlockSpec((1, d), lambda i: (0, 0))
    return pl.pallas_call(
        _normmod_kernel,
        grid=(m // tr,),
        in_specs=[pl.BlockSpec((tr, d), lambda i: (i, 0)), row, row, row],
        out_specs=pl.BlockSpec((tr, d), lambda i: (i, 0)),
        out_shape=jax.ShapeDtypeStruct((m, d), BF16),
        compiler_params=_params(("parallel",), 6 * tr * d * 4),
        name="norm_modulate",
    )(x, g, shift, scale)


def _rmsnorm(x, g, tr=256):
    m, d = x.shape
    return pl.pallas_call(
        _rmsnorm_kernel,
        grid=(m // tr,),
        in_specs=[pl.BlockSpec((tr, d), lambda i: (i, 0)), pl.BlockSpec((1, d), lambda i: (0, 0))],
        out_specs=pl.BlockSpec((tr, d), lambda i: (i, 0)),
        out_shape=jax.ShapeDtypeStruct((m, d), F32),
        compiler_params=_params(("parallel",), 6 * tr * d * 4),
        name="final_norm",
    )(x, g)


MXU_N = 256


def _dot(a, b):
    return jnp.dot(a, b, preferred_element_type=F32)


NORM_MIN_ROWS = 64


def _call(body, grid, in_specs, args, out_spec, out_shape, vmem, name, cast_jobs=(), norm_a=None):
    if norm_a is not None:
        x_arr = norm_a[0]
        n_i, n_j = grid
        tm, kdim = x_arr.shape[0] // n_i, x_arr.shape[1]
        nb = 1
        while nb * 2 <= min(n_j, tm // NORM_MIN_ROWS):
            nb *= 2
        rb = tm // nb
        assert rb * nb == tm and x_arr.shape[0] == tm * n_i
        grid = (n_i + 1, n_j)
        lag = lambda f: (lambda i, j: f(jnp.maximum(i - 1, 0), j))
        in_specs = [pl.BlockSpec(s.block_shape, lag(s.index_map), pipeline_mode=s.pipeline_mode)
                    for s in in_specs]
        out_spec = pl.BlockSpec(out_spec.block_shape, lag(out_spec.index_map))
        rowspec = pl.BlockSpec((1, kdim), lambda i, j: (0, 0))
        xspec = pl.BlockSpec(
            (rb, kdim), lambda i, j: (jnp.minimum(i, n_i - 1) * nb + jnp.minimum(j, nb - 1), 0))
        in_specs = in_specs + [xspec, rowspec, rowspec, rowspec]
        args = list(args) + list(norm_a)
        vmem += 6 * rb * kdim * 4
    steps = 1
    for g in grid:
        steps *= g
    job_in, job_out, job_shapes, job_args = [], [], [], []
    for src, layer in cast_jobs:
        r, c = src.shape[-2:]
        br = next(b for b in (16, 32, 64, 128, 256, 512, 1024, 2048, 4096) if r % b == 0 and r // b <= steps)
        nblk = r // br

        def blk(*g, nblk=nblk):
            s = g[0]
            for gi, n in zip(g[1:], grid[1:]):
                s = s * n + gi
            return jnp.minimum(s, nblk - 1)

        if layer is None:
            job_in.append(pl.BlockSpec((br, c), lambda *g, blk=blk: (blk(*g), 0)))
        else:
            job_in.append(pl.BlockSpec((None, br, c), lambda *g, blk=blk, layer=layer: (layer, blk(*g), 0)))
        job_out.append(pl.BlockSpec((br, c), lambda *g, blk=blk: (blk(*g), 0)))
        job_shapes.append(jax.ShapeDtypeStruct((r, c), BF16))
        job_args.append(src)
        vmem += 2 * br * c * (4 + 2)
    n_in, n_jobs = len(in_specs), len(cast_jobs)

    def kern(*refs):
        ins = refs[:n_in]
        jin = refs[n_in:n_in + n_jobs]
        out = refs[n_in + n_jobs]
        jout = refs[n_in + n_jobs + 1:n_in + 2 * n_jobs + 1]

        def casts():
            for s_ref, d_ref in zip(jin, jout):
                d_ref[...] = s_ref[...].astype(d_ref.dtype)

        if norm_a is None:
            casts()
            body(*ins, out)
            return

        tiles = refs[-2:]
        x_ref, g_ref, sh_ref, sc_ref = ins[-4:]
        i = pl.program_id(0)
        j = pl.program_id(1)

        def prepare(dst):
            x = x_ref[...]
            r = lax.rsqrt(jnp.mean(x * x, axis=-1, keepdims=True) + EPS)
            y = (x * r) * g_ref[...]
            h = (y * (1.0 + sc_ref[...]) + sh_ref[...]).astype(BF16)
            r0 = pl.multiple_of(jnp.minimum(j, nb - 1) * rb, rb)
            dst[pl.ds(r0, rb), :] = h

        @pl.when(i == 0)
        def _():
            prepare(tiles[0])
            casts()
            out[...] = jnp.zeros(out.shape, out.dtype)

        for parity in (0, 1):
            @pl.when((i > 0) & (i % 2 == parity))
            def _(parity=parity):
                prepare(tiles[parity])
                casts()
                body(tiles[1 - parity], *ins[:-4], out)

    scratch = [] if norm_a is None else [pltpu.VMEM((tm, kdim), BF16), pltpu.VMEM((tm, kdim), BF16)]
    res = pl.pallas_call(
        kern,
        grid=grid,
        in_specs=list(in_specs) + job_in,
        out_specs=[out_spec] + job_out,
        out_shape=[out_shape] + job_shapes,
        scratch_shapes=scratch,
        compiler_params=_params(("arbitrary",) * len(grid), vmem),
        name=name,
    )(*args, *job_args)
    return res if cast_jobs else res[0]


def _mm_plain_body(a_ref, w_ref, o_ref):
    a = a_ref[...]
    for c0 in range(0, o_ref.shape[1], MXU_N):
        o_ref[:, c0:c0 + MXU_N] = _dot(a, w_ref[:, c0:c0 + MXU_N]).astype(o_ref.dtype)


def _mm_plain(a, w, col0, n, out_dtype, tm, tn, cast_jobs=()):
    m, k = a.shape
    tm = min(tm, m)
    jb = col0 // tn
    osz = jnp.dtype(out_dtype).itemsize
    vm = 2 * (tm * k * 2 + k * tn * 2 + tm * tn * osz) + 2 * tm * MXU_N * 4
    return _call(
        _mm_plain_body, (m // tm, n // tn),
        [pl.BlockSpec((tm, k), lambda i, j: (i, 0)), pl.BlockSpec((k, tn), lambda i, j: (0, j + jb))],
        (a, w), pl.BlockSpec((tm, tn), lambda i, j: (i, j)), jax.ShapeDtypeStruct((m, n), out_dtype),
        vm, "matmul_plain", cast_jobs)


def _mm_rope_body(a_ref, w_ref, cos_ref, sin_ref, o_ref, *, n_rope_tiles, n_q_tiles):
    j = pl.program_id(1)
    tn = o_ref.shape[1]
    half = HEAD_DIM // 2

    @pl.when(j < n_rope_tiles)
    def _():
        a = a_ref[...]
        cos = cos_ref[...]
        sin = sin_ref[...]
        scale = jnp.where(j >= n_q_tiles, QK_SCALE, 1.0).astype(F32)
        for lo in range(0, tn, HEAD_DIM):
            acc = _dot(a, w_ref[:, lo:lo + HEAD_DIM])
            t1 = acc[:, :half]
            t2 = acc[:, half:]
            o_ref[:, lo:lo + half] = ((t1 * cos - t2 * sin) * scale).astype(o_ref.dtype)
            o_ref[:, lo + half:lo + HEAD_DIM] = ((t1 * sin + t2 * cos) * scale).astype(o_ref.dtype)

    @pl.when(j >= n_rope_tiles)
    def _():
        _mm_plain_body(a_ref, w_ref, o_ref)


def _mm_rope(h, w, cos, sin, tm=1024, tn=1024, cast_jobs=()):
    m, k = h.shape
    n = 3 * D_RET
    body = functools.partial(_mm_rope_body, n_rope_tiles=2 * D_RET // tn, n_q_tiles=D_RET // tn)
    vm = 2 * (tm * k * 2 + k * tn * 2 + tm * tn * 2 + 2 * tm * LANES * 4) + 4 * tm * MXU_N * 4
    tab = pl.BlockSpec((tm, HEAD_DIM // 2), lambda i, j: (i, 0))
    return _call(
        body, (m // tm, n // tn),
        [pl.BlockSpec((tm, k), lambda i, j: (i, 0)), pl.BlockSpec((k, tn), lambda i, j: (0, j)), tab, tab],
        (h, w, cos, sin), pl.BlockSpec((tm, tn), lambda i, j: (i, j)), jax.ShapeDtypeStruct((m, n), BF16),
        vm, "matmul_qkv_rope", cast_jobs)


def _mm_glu_body(a_ref, wa_ref, wg_ref, ba_ref, bg_ref, o_ref):
    a = a_ref[...]
    for c0 in range(0, o_ref.shape[1], MXU_N):
        cs = slice(c0, c0 + MXU_N)
        p = _dot(a, wa_ref[:, cs]) + ba_ref[:, cs]
        q = _dot(a, wg_ref[:, cs]) + bg_ref[:, cs]
        o_ref[:, cs] = (p * jax.nn.sigmoid(q)).astype(o_ref.dtype)


def _mm_glu(norm_a, w, b, n, out_dtype, tm=1024, tn=512, cast_jobs=()):
    m, k = norm_a[0].shape
    jb = n // tn
    osz = jnp.dtype(out_dtype).itemsize
    vm = 2 * (tm * k * 2 + 2 * k * tn * 2 + tm * tn * osz) + 4 * tm * MXU_N * 4
    return _call(
        _mm_glu_body, (m // tm, n // tn),
        [pl.BlockSpec((k, tn), lambda i, j: (0, j)),
         pl.BlockSpec((k, tn), lambda i, j: (0, j + jb)),
         pl.BlockSpec((1, tn), lambda i, j: (0, j)),
         pl.BlockSpec((1, tn), lambda i, j: (0, j + jb))],
        (w, w, b, b), pl.BlockSpec((tm, tn), lambda i, j: (i, j)), jax.ShapeDtypeStruct((m, n), out_dtype),
        vm, "matmul_glu", cast_jobs, norm_a)


SWIGLU_ROWS = 1024


def _mm_swiglu_body(a_ref, w1_ref, w3_ref, o_ref):
    for r0 in range(0, a_ref.shape[0], SWIGLU_ROWS):
        rs = slice(r0, r0 + SWIGLU_ROWS)
        a = a_ref[rs, :]
        p = _dot(a, w1_ref[...])
        q = _dot(a, w3_ref[...])
        o_ref[rs, :] = (_silu(p) * q).astype(o_ref.dtype)


def _mm_swiglu(norm_a, w1, w3, tm=2048, tn=256, cast_jobs=()):
    m, k = norm_a[0].shape
    n = w1.shape[1]
    vm = 2 * (tm * k * 2 + 2 * k * tn * 2 + tm * tn * 2) + 4 * SWIGLU_ROWS * tn * 4
    wspec = pl.BlockSpec((k, tn), lambda i, j: (0, j))
    return _call(
        _mm_swiglu_body, (m // tm, n // tn), [wspec, wspec],
        (w1, w3), pl.BlockSpec((tm, tn), lambda i, j: (i, j)), jax.ShapeDtypeStruct((m, n), BF16),
        vm, "matmul_swiglu", cast_jobs, norm_a)


def _mm_resid_body(*refs, n_a, has_bias):
    a_refs = refs[:n_a]
    w_ref = refs[n_a]
    b_ref = refs[n_a + 1] if has_bias else None
    g_ref, x_ref, o_ref = refs[n_a + has_bias + 1:]
    a_vals = [r[...] for r in a_refs]
    for c0 in range(0, o_ref.shape[1], MXU_N):
        cs = slice(c0, c0 + MXU_N)
        k0 = 0
        acc = None
        for a in a_vals:
            part = _dot(a, w_ref[k0:k0 + a.shape[1], cs])
            acc = part if acc is None else acc + part
            k0 += a.shape[1]
        if has_bias:
            acc = acc + b_ref[:, cs]
        o_ref[:, cs] = x_ref[:, cs] + g_ref[:, cs] * acc


def _mm_resid(a_list, w, gate, xres, bias=None, tm=1024, tn=512, w_resident=False, cast_jobs=()):
    m = a_list[0].shape[0]
    k, n = w.shape
    vm = 2 * (tm * k * 2 + 2 * tm * tn * 4) + (1 if w_resident else 2) * k * tn * 2 + 2 * tm * MXU_N * 4
    if w_resident:
        grid = (n // tn, m // tm)
        ij = lambda f: (lambda j, i: f(i, j))
        wspec = pl.BlockSpec((k, tn), ij(lambda i, j: (0, j)), pipeline_mode=pl.Buffered(1))
    else:
        grid = (m // tm, n // tn)
        ij = lambda f: f
        wspec = pl.BlockSpec((k, tn), lambda i, j: (0, j))
    rowspec = pl.BlockSpec((1, tn), ij(lambda i, j: (0, j)))
    tile = pl.BlockSpec((tm, tn), ij(lambda i, j: (i, j)))
    in_specs = [pl.BlockSpec((tm, a.shape[1]), ij(lambda i, j: (i, 0))) for a in a_list] + [wspec]
    args = list(a_list) + [w]
    if bias is not None:
        in_specs.append(rowspec)
        args.append(bias)
    in_specs += [rowspec, tile]
    args += [gate, xres]
    body = functools.partial(_mm_resid_body, n_a=len(a_list), has_bias=bias is not None)
    return _call(body, grid, in_specs, args, tile, jax.ShapeDtypeStruct((m, n), F32), vm,
                 "matmul_resid_wres" if w_resident else "matmul_resid", cast_jobs)


def _ret_kernel(*refs, backward):
    if backward:
        ld_ref, q_ref, k_ref, v_ref, kc_ref, vc_ref, of_ref, g_ref, o_ref, s_ref = refs
    else:
        ld_ref, q_ref, k_ref, v_ref, kc_ref, vc_ref, o_ref, s_ref = refs
    c = RET_CHUNK
    hd = HEAD_DIM
    t = pl.program_id(1)
    ri = lax.broadcasted_iota(jnp.int32, (c, c), 0)
    ci = lax.broadcasted_iota(jnp.int32, (c, c), 1)
    dist = ((ci - ri) if backward else (ri - ci)).astype(F32)
    pos = lax.broadcasted_iota(jnp.int32, (c, 1), 0).astype(F32)
    tdims = (((0,), (0,)), ((), ()))
    n_chunks = q_ref.shape[0] // c
    order = range(n_chunks - 1, -1, -1) if backward else range(n_chunks)

    heads = []
    for hh in range(RET_HEADS):
        lg = -jnp.exp(ld_ref[hh, 0:1, 0:1])
        dmat = jnp.where(dist >= 0, jnp.exp(lg * jnp.maximum(dist, 0.0)), 0.0)
        if backward:
            qdec = jnp.exp(lg * (c - pos))
            kdec = jnp.exp(lg * pos)
        else:
            qdec = jnp.exp(lg * (pos + 1.0))
            kdec = jnp.exp(lg * (c - 1.0 - pos))
        cdec = jnp.exp(lg * float(c))
        heads.append((lg, dmat, qdec, kdec, cdec))

    @pl.when(t == 0)
    def _():
        lc = kc_ref.shape[0]
        jpos = lax.broadcasted_iota(jnp.int32, (lc, 1), 0).astype(F32)
        for hh in range(RET_HEADS):
            cs = slice(hh * hd, (hh + 1) * hd)
            w = jnp.exp(heads[hh][0] * (jpos if backward else (lc - 1.0 - jpos)))
            kcw = ((kc_ref[:, cs] * QK_SCALE) * w).astype(BF16)
            s_ref[hh] = lax.dot_general(kcw, vc_ref[:, cs].astype(BF16), tdims, preferred_element_type=F32)

    for cidx in order:
        rows = pl.ds(cidx * c, c)
        for hh in range(RET_HEADS):
            _, dmat, qdec, kdec, cdec = heads[hh]
            cs = slice(hh * hd, (hh + 1) * hd)
            q = q_ref[rows, cs]
            k = k_ref[rows, cs]
            v = v_ref[rows, cs]
            s = s_ref[hh]
            scores = lax.dot_general(q, k, (((1,), (1,)), ((), ())), preferred_element_type=F32) * dmat
            intra = jnp.dot(scores.astype(BF16), v, preferred_element_type=F32)
            q_in = (q.astype(F32) * qdec).astype(BF16)
            cross = jnp.dot(q_in, s.astype(BF16), preferred_element_type=F32)
            k_in = (k.astype(F32) * kdec).astype(BF16)
            s_ref[hh] = cdec * s + lax.dot_general(k_in, v, tdims, preferred_element_type=F32)
            o = intra + cross
            if backward:
                y = of_ref[rows, cs] + o
                mu = jnp.mean(y, axis=-1, keepdims=True)
                yc = y - mu
                yn = yc * lax.rsqrt(jnp.mean(yc * yc, axis=-1, keepdims=True) + EPS)
                o_ref[rows, cs] = (_silu(g_ref[rows, cs]) * yn).astype(o_ref.dtype)
            else:
                o_ref[rows, cs] = o


def _retention(ld, qkv, kvc, ofwd=None, rest=None):
    backward = ofwd is not None
    seq = qkv.shape[0]
    nt = seq // RET_ROWS
    tt = (lambda t: nt - 1 - t) if backward else (lambda t: t)
    dirn = 1 if backward else 0
    hd = HEAD_DIM
    wd = RET_HEADS * hd
    ng = N_HEADS // RET_HEADS
    blk = lambda off: pl.BlockSpec((RET_ROWS, wd), lambda h, t: (tt(t), h + off))
    in_specs = [
        pl.BlockSpec((None, RET_HEADS, 8, LANES), lambda h, t: (dirn, h, 0, 0)),
        blk(0), blk(ng), blk(2 * ng),
        pl.BlockSpec((CTX_LEN, wd), lambda h, t: (0, h)),
        pl.BlockSpec((CTX_LEN, wd), lambda h, t: (0, h + ng)),
    ]
    args = [ld, qkv, qkv, qkv, kvc, kvc]
    if backward:
        in_specs += [blk(0), blk(0)]
        args += [ofwd, rest]
    out_dtype = BF16 if backward else F32
    tile = RET_ROWS * wd
    vm = 2 * (3 * tile * 2 + 2 * CTX_LEN * wd * 4 + (2 * tile * 4 if backward else 0)
              + tile * jnp.dtype(out_dtype).itemsize) + 8 * RET_HEADS * RET_CHUNK * RET_CHUNK * 4
    return pl.pallas_call(
        functools.partial(_ret_kernel, backward=backward),
        grid=(ng, nt),
        in_specs=in_specs,
        out_specs=blk(0),
        out_shape=jax.ShapeDtypeStruct((seq, D_RET), out_dtype),
        scratch_shapes=[pltpu.VMEM((RET_HEADS, hd, hd), F32)],
        compiler_params=_params(("arbitrary", "arbitrary"), vm),
        name="retention_bwd" if backward else "retention_fwd",
    )(*args)


def _sconv_kernel(bg_ref, cg_ref, xin_ref, cgp_ref, xinp_ref, cgn_ref, xinn_ref, w_ref, o_ref, p_ref):
    i = pl.program_id(0)
    tr = cg_ref.shape[0]
    p_ref[8:8 + tr, :] = cg_ref[...] * xin_ref[...]
    p_ref[0:8, :] = jnp.where(i > 0, cgp_ref[...] * xinp_ref[...], 0.0)
    p_ref[8 + tr:16 + tr, :] = jnp.where(i < pl.num_programs(0) - 1, cgn_ref[...] * xinn_ref[...], 0.0)
    conv = (w_ref[0:1, :] * p_ref[7:7 + tr, :] + w_ref[1:2, :] * p_ref[8:8 + tr, :]
            + w_ref[2:3, :] * p_ref[9:9 + tr, :])
    o_ref[...] = (bg_ref[...] * conv).astype(o_ref.dtype)


def _short_conv(rest, sc_w, tr=512, tc=1024):
    seq = rest.shape[0]
    nct = D_SC // tc
    r8 = tr // 8
    last8 = seq // 8 - 1
    main = lambda part: pl.BlockSpec((tr, tc), lambda i, j: (i, j + part * nct))
    prev = lambda part: pl.BlockSpec((8, tc), lambda i, j: (jnp.maximum(i * r8 - 1, 0), j + part * nct))
    nxt = lambda part: pl.BlockSpec((8, tc), lambda i, j: (jnp.minimum((i + 1) * r8, last8), j + part * nct))
    return pl.pallas_call(
        _sconv_kernel,
        grid=(seq // tr, nct),
        in_specs=[main(1), main(2), main(3), prev(2), prev(3), nxt(2), nxt(3),
                  pl.BlockSpec((SC_WIDTH, tc), lambda i, j: (0, j))],
        out_specs=pl.BlockSpec((tr, tc), lambda i, j: (i, j)),
        out_shape=jax.ShapeDtypeStruct((seq, D_SC), BF16),
        scratch_shapes=[pltpu.VMEM((tr + 16, tc), F32)],
        compiler_params=_params(("arbitrary", "arbitrary"), 10 * tr * tc * 4),
        name="short_conv",
    )(rest, rest, rest, rest, rest, rest, rest, sc_w)


CONF_HALO = 16
CONF_ROW_CHUNK = 64


def _dwconv_kernel(u_ref, up_ref, un_ref, w_ref, b_ref, g_ref, beta_ref, o_ref, p_ref, c_ref):
    i = pl.program_id(0)
    tr, d = u_ref.shape
    hl = CONF_HALO
    pad = (CONF_KERNEL - 1) // 2
    p_ref[hl:hl + tr, :] = u_ref[...]
    p_ref[0:hl, :] = jnp.where(i > 0, up_ref[...], 0.0)
    p_ref[hl + tr:2 * hl + tr, :] = jnp.where(i < pl.num_programs(0) - 1, un_ref[...], 0.0)
    rc = CONF_ROW_CHUNK

    off = hl - pad

    def strip_body(s, carry):
        cols = pl.ds(pl.multiple_of(s * LANES, LANES), LANES)
        wv = w_ref[:, cols]
        bias = b_ref[:, cols]
        for r0 in range(0, tr, rc):
            out = None
            for e in range(8):
                q = None
                for j in range(CONF_KERNEL):
                    if (off + j) % 8 != e:
                        continue
                    a8 = r0 + (off + j) - e
                    term = wv[j:j + 1, :] * p_ref[a8:a8 + rc + 8, cols]
                    q = term if q is None else q + term
                part = q[e:e + rc, :]
                out = part if out is None else out + part
            c_ref[r0:r0 + rc, cols] = out + bias
        return carry

    lax.fori_loop(0, d // LANES, strip_body, 0)
    x = c_ref[...]
    mu = jnp.mean(x, axis=-1, keepdims=True)
    xc = x - mu
    y = xc * lax.rsqrt(jnp.mean(xc * xc, axis=-1, keepdims=True) + EPS)
    y = y * g_ref[...] + beta_ref[...]
    o_ref[...] = _silu(y).astype(o_ref.dtype)


def _dwconv_ln_silu(u, dw_w, dw_b, ln_g, ln_b, tr=256):
    seq, d = u.shape
    hl = CONF_HALO
    rh = tr // hl
    lasth = seq // hl - 1
    row = pl.BlockSpec((1, d), lambda i: (0, 0))
    return pl.pallas_call(
        _dwconv_kernel,
        grid=(seq // tr,),
        in_specs=[pl.BlockSpec((tr, d), lambda i: (i, 0)),
                  pl.BlockSpec((hl, d), lambda i: (jnp.maximum(i * rh - 1, 0), 0)),
                  pl.BlockSpec((hl, d), lambda i: (jnp.minimum((i + 1) * rh, lasth), 0)),
                  pl.BlockSpec((CONF_KERNEL, d), lambda i: (0, 0)),
                  row, row, row],
        out_specs=pl.BlockSpec((tr, d), lambda i: (i, 0)),
        out_shape=jax.ShapeDtypeStruct((seq, d), BF16),
        scratch_shapes=[pltpu.VMEM((tr + 2 * hl, d), F32), pltpu.VMEM((tr, d), F32)],
        compiler_params=_params(("arbitrary",), 8 * tr * d * 4),
        name="dwconv_ln_silu",
    )(u, u, u, dw_w, dw_b, ln_g, ln_b)


def _rope_tables(seq):
    n_rows = seq // GRID_W
    rows = jnp.broadcast_to(jnp.arange(n_rows, dtype=F32)[:, None], (n_rows, GRID_W)).reshape(-1)
    cols = jnp.broadcast_to(jnp.arange(GRID_W, dtype=F32)[None, :], (n_rows, GRID_W)).reshape(-1)
    n_freq = HEAD_DIM // 4
    freqs = ROPE_BASE ** (-jnp.arange(n_freq, dtype=F32) / n_freq)
    ang = jnp.concatenate([rows[:, None] * freqs, cols[:, None] * freqs], axis=-1)
    return jnp.cos(ang), jnp.sin(ang)


def kernel(x, c, ctx, c_ctx, ada_w, ada_b, norm_mix, norm_ffn, norm_final, even_w_in, ret_log_decay, sc_conv_w, even_w_out, conf_w_pw1, conf_b_pw1, conf_dw_w, conf_dw_b, conf_ln_g, conf_ln_b, conf_w_pw2, conf_b_pw2, ffn_w1, ffn_w3, ffn_w2):
    d = D_MODEL
    seq = x.shape[1]
    xs = x[0]
    row = lambda v: v.reshape(1, -1)

    cc8 = jnp.zeros((8, d), F32).at[0].set(c[0]).at[1].set(c_ctx)
    mods = _ada(cc8, ada_w, ada_b)
    cos, sin = _rope_tables(seq)
    mod_rows = lambda i: [row(mods[i, 0][n * d:(n + 1) * d]) for n in range(N_MOD)]

    sh1, sc1, g1, sh2, sc2, g2 = mod_rows(0)
    h = _normmod(xs, row(norm_mix[0]), sh1, sc1)
    mc = mods[0, 1]
    hc = _normmod(ctx[0], row(norm_mix[0]), row(mc[:d]), row(mc[d:2 * d]))
    w_in = even_w_in[0].astype(BF16)
    qkv, w_out, w1_0 = _mm_rope(h, w_in, cos, sin, cast_jobs=((even_w_out, 0), (ffn_w1, 0)))
    rest, w3_0, w2_0 = _mm_plain(h, w_in, 3 * D_RET, D_RET + 3 * D_SC, F32, 1024, 1024,
                                 cast_jobs=((ffn_w3, 0), (ffn_w2, 0)))
    kvc = _mm_plain(hc, w_in, D_RET, 2 * D_RET, F32, CTX_LEN, 1024)
    ld = jnp.broadcast_to(ret_log_decay[0][:, :, None, None], (2, N_HEADS, 8, LANES))
    ofwd = _retention(ld, qkv, kvc)
    ret = _retention(ld, qkv, kvc, ofwd, rest)
    sc = _short_conv(rest, sc_conv_w[0])
    xs = _mm_resid([ret, sc], w_out, g1, xs, tn=1024)
    u, w1_1, w3_1 = _mm_swiglu((xs, row(norm_ffn[0]), sh2, sc2), w1_0, w3_0,
                               cast_jobs=((ffn_w1, 1), (ffn_w3, 1)))
    xs, w2_1, w_pw1, w_pw2 = _mm_resid(
        [u], w2_0, g2, xs, tm=256, tn=1024, w_resident=True,
        cast_jobs=((ffn_w2, 1), (conf_w_pw1, 0), (conf_w_pw2, 0)))

    sh1, sc1, g1, sh2, sc2, g2 = mod_rows(1)
    u = _mm_glu((xs, row(norm_mix[1]), sh1, sc1), w_pw1, row(conf_b_pw1[0]), d, F32)
    v = _dwconv_ln_silu(u, conf_dw_w[0], row(conf_dw_b[0]), row(conf_ln_g[0]), row(conf_ln_b[0]))
    xs = _mm_resid([v], w_pw2, g1, xs, bias=row(conf_b_pw2[0]), tn=1024)
    u = _mm_swiglu((xs, row(norm_ffn[1]), sh2, sc2), w1_1, w3_1)
    xs = _mm_resid([u], w2_1, g2, xs, tm=512, tn=1024, w_resident=True)
    out = _rmsnorm(xs, row(norm_final))
    return out[None]
```

```python
import functools

import jax
import jax.numpy as jnp
from jax import lax
from jax.experimental import pallas as pl
from jax.experimental.pallas import tpu as pltpu

F32 = jnp.float32
BF16 = jnp.bfloat16

D_MODEL = 4096
SEQ = 16384
GRID_W = 64
CTX_LEN = 256
D_RET = D_MODEL // 2
HEAD_DIM = 256
N_HEADS = D_RET // HEAD_DIM
D_SC = D_MODEL - D_RET
SC_WIDTH = 3
ROPE_BASE = 10000.0
CONF_KERNEL = 31
FFN_HIDDEN = 11008
N_MOD = 6
EPS = 1e-6
QK_SCALE = HEAD_DIM ** -0.5

V7X_VMEM_BYTES = 64 * 1024 * 1024
VMEM_CEILING = V7X_VMEM_BYTES - 6 * 1024 * 1024
LANES = 128
RET_CHUNK = 256
RET_ROWS = 1024
RET_HEADS = 4


def _params(sem, vmem_bytes):
    limit = min(int(vmem_bytes * 1.2) + (4 << 20), VMEM_CEILING)
    return pltpu.CompilerParams(dimension_semantics=sem, vmem_limit_bytes=limit)


def _silu(v):
    return v * jax.nn.sigmoid(v)


def _ada_kernel(cc_ref, w_ref, b_ref, o_ref):
    s = _silu(cc_ref[...]).astype(BF16)
    acc = jnp.dot(s, w_ref[...].astype(BF16), preferred_element_type=F32)
    o_ref[...] = acc + b_ref[...]


def _ada(cc8, ada_w, ada_b):
    depth, d, n = ada_w.shape
    tn = 512
    return pl.pallas_call(
        _ada_kernel,
        grid=(depth, n // tn),
        in_specs=[
            pl.BlockSpec((8, d), lambda l, j: (0, 0)),
            pl.BlockSpec((None, d, tn), lambda l, j: (l, 0, j)),
            pl.BlockSpec((None, 1, tn), lambda l, j: (l, 0, j)),
        ],
        out_specs=pl.BlockSpec((None, 8, tn), lambda l, j: (l, 0, j)),
        out_shape=jax.ShapeDtypeStruct((depth, 8, n), F32),
        compiler_params=_params(("arbitrary", "arbitrary"), 3 * d * tn * 4),
        name="ada_mod",
    )(cc8, ada_w, ada_b.reshape(depth, 1, n))


def _normmod_kernel(x_ref, g_ref, sh_ref, sc_ref, o_ref):
    x = x_ref[...]
    r = lax.rsqrt(jnp.mean(x * x, axis=-1, keepdims=True) + EPS)
    y = (x * r) * g_ref[...]
    o_ref[...] = (y * (1.0 + sc_ref[...]) + sh_ref[...]).astype(o_ref.dtype)


def _rmsnorm_kernel(x_ref, g_ref, o_ref):
    x = x_ref[...]
    r = lax.rsqrt(jnp.mean(x * x, axis=-1, keepdims=True) + EPS)
    o_ref[...] = ((x * r) * g_ref[...]).astype(o_ref.dtype)


def _normmod(x, g, shift, scale, tr=256):
    m, d = x.shape
    tr = min(tr, m)
    row = pl.BlockSpec((1, d), lambda i: (0, 0))
    return pl.pallas_call(
        _normmod_kernel,
        grid=(m // tr,),
        in_specs=[pl.BlockSpec((tr, d), lambda i: (i, 0)), row, row, row],
        out_specs=pl.BlockSpec((tr, d), lambda i: (i, 0)),
        out_shape=jax.ShapeDtypeStruct((m, d), BF16),
        compiler_params=_params(("parallel",), 6 * tr * d * 4),
        name="norm_modulate",
    )(x, g, shift, scale)


def _rmsnorm(x, g, tr=256):
    m, d = x.shape
    return pl.pallas_call(
        _rmsnorm_kernel,
        grid=(m // tr,),
        in_specs=[pl.BlockSpec((tr, d), lambda i: (i, 0)), pl.BlockSpec((1, d), lambda i: (0, 0))],
        out_specs=pl.BlockSpec((tr, d), lambda i: (i, 0)),
        out_shape=jax.ShapeDtypeStruct((m, d), F32),
        compiler_params=_params(("parallel",), 6 * tr * d * 4),
        name="final_norm",
    )(x, g)


MXU_N = 256


def _dot(a, b):
    return jnp.dot(a, b, preferred_element_type=F32)


def _call(body, grid, in_specs, args, out_spec, out_shape, vmem, name, cast_jobs=()):
    steps = 1
    for g in grid:
        steps *= g
    job_in, job_out, job_shapes, job_args = [], [], [], []
    for src, layer in cast_jobs:
        r, c = src.shape[-2:]
        br = next(b for b in (16, 32, 64, 128, 256, 512, 1024, 2048, 4096) if r % b == 0 and r // b <= steps)
        nblk = r // br

        def blk(*g, nblk=nblk):
            s = g[0]
            for gi, n in zip(g[1:], grid[1:]):
                s = s * n + gi
            return jnp.minimum(s, nblk - 1)

        if layer is None:
            job_in.append(pl.BlockSpec((br, c), lambda *g, blk=blk: (blk(*g), 0)))
        else:
            job_in.append(pl.BlockSpec((None, br, c), lambda *g, blk=blk, layer=layer: (layer, blk(*g), 0)))
        job_out.append(pl.BlockSpec((br, c), lambda *g, blk=blk: (blk(*g), 0)))
        job_shapes.append(jax.ShapeDtypeStruct((r, c), BF16))
        job_args.append(src)
        vmem += 2 * br * c * (4 + 2)
    n_in, n_jobs = len(in_specs), len(cast_jobs)

    def kern(*refs):
        ins = refs[:n_in]
        jin = refs[n_in:n_in + n_jobs]
        out = refs[n_in + n_jobs]
        jout = refs[n_in + n_jobs + 1:]
        for s_ref, d_ref in zip(jin, jout):
            d_ref[...] = s_ref[...].astype(d_ref.dtype)
        body(*ins, out)

    res = pl.pallas_call(
        kern,
        grid=grid,
        in_specs=list(in_specs) + job_in,
        out_specs=[out_spec] + job_out,
        out_shape=[out_shape] + job_shapes,
        compiler_params=_params(("arbitrary",) * len(grid), vmem),
        name=name,
    )(*args, *job_args)
    return res if cast_jobs else res[0]


def _mm_plain_body(a_ref, w_ref, o_ref):
    a = a_ref[...]
    for c0 in range(0, o_ref.shape[1], MXU_N):
        o_ref[:, c0:c0 + MXU_N] = _dot(a, w_ref[:, c0:c0 + MXU_N]).astype(o_ref.dtype)


def _mm_plain(a, w, col0, n, out_dtype, tm, tn, cast_jobs=()):
    m, k = a.shape
    tm = min(tm, m)
    jb = col0 // tn
    osz = jnp.dtype(out_dtype).itemsize
    vm = 2 * (tm * k * 2 + k * tn * 2 + tm * tn * osz) + 2 * tm * MXU_N * 4
    return _call(
        _mm_plain_body, (m // tm, n // tn),
        [pl.BlockSpec((tm, k), lambda i, j: (i, 0)), pl.BlockSpec((k, tn), lambda i, j: (0, j + jb))],
        (a, w), pl.BlockSpec((tm, tn), lambda i, j: (i, j)), jax.ShapeDtypeStruct((m, n), out_dtype),
        vm, "matmul_plain", cast_jobs)


def _mm_rope_body(a_ref, w_ref, cos_ref, sin_ref, o_ref, *, n_rope_tiles, n_q_tiles):
    j = pl.program_id(1)
    tn = o_ref.shape[1]
    half = HEAD_DIM // 2

    @pl.when(j < n_rope_tiles)
    def _():
        a = a_ref[...]
        cos = cos_ref[...]
        sin = sin_ref[...]
        scale = jnp.where(j >= n_q_tiles, QK_SCALE, 1.0).astype(F32)
        for lo in range(0, tn, HEAD_DIM):
            acc = _dot(a, w_ref[:, lo:lo + HEAD_DIM])
            t1 = acc[:, :half]
            t2 = acc[:, half:]
            o_ref[:, lo:lo + half] = ((t1 * cos - t2 * sin) * scale).astype(o_ref.dtype)
            o_ref[:, lo + half:lo + HEAD_DIM] = ((t1 * sin + t2 * cos) * scale).astype(o_ref.dtype)

    @pl.when(j >= n_rope_tiles)
    def _():
        _mm_plain_body(a_ref, w_ref, o_ref)


def _mm_rope(h, w, cos, sin, tm=1024, tn=1024, cast_jobs=()):
    m, k = h.shape
    n = 3 * D_RET
    body = functools.partial(_mm_rope_body, n_rope_tiles=2 * D_RET // tn, n_q_tiles=D_RET // tn)
    vm = 2 * (tm * k * 2 + k * tn * 2 + tm * tn * 2 + 2 * tm * LANES * 4) + 4 * tm * MXU_N * 4
    tab = pl.BlockSpec((tm, HEAD_DIM // 2), lambda i, j: (i, 0))
    return _call(
        body, (m // tm, n // tn),
        [pl.BlockSpec((tm, k), lambda i, j: (i, 0)), pl.BlockSpec((k, tn), lambda i, j: (0, j)), tab, tab],
        (h, w, cos, sin), pl.BlockSpec((tm, tn), lambda i, j: (i, j)), jax.ShapeDtypeStruct((m, n), BF16),
        vm, "matmul_qkv_rope", cast_jobs)


def _mm_glu_body(a_ref, wa_ref, wg_ref, ba_ref, bg_ref, o_ref):
    a = a_ref[...]
    for c0 in range(0, o_ref.shape[1], MXU_N):
        cs = slice(c0, c0 + MXU_N)
        p = _dot(a, wa_ref[:, cs]) + ba_ref[:, cs]
        q = _dot(a, wg_ref[:, cs]) + bg_ref[:, cs]
        o_ref[:, cs] = (p * jax.nn.sigmoid(q)).astype(o_ref.dtype)


def _mm_glu(a, w, b, n, out_dtype, tm=1024, tn=512, cast_jobs=()):
    m, k = a.shape
    jb = n // tn
    osz = jnp.dtype(out_dtype).itemsize
    vm = 2 * (tm * k * 2 + 2 * k * tn * 2 + tm * tn * osz) + 4 * tm * MXU_N * 4
    return _call(
        _mm_glu_body, (m // tm, n // tn),
        [pl.BlockSpec((tm, k), lambda i, j: (i, 0)),
         pl.BlockSpec((k, tn), lambda i, j: (0, j)),
         pl.BlockSpec((k, tn), lambda i, j: (0, j + jb)),
         pl.BlockSpec((1, tn), lambda i, j: (0, j)),
         pl.BlockSpec((1, tn), lambda i, j: (0, j + jb))],
        (a, w, w, b, b), pl.BlockSpec((tm, tn), lambda i, j: (i, j)), jax.ShapeDtypeStruct((m, n), out_dtype),
        vm, "matmul_glu", cast_jobs)


SWIGLU_ROWS = 1024


def _mm_swiglu_body(a_ref, w1_ref, w3_ref, o_ref):
    for r0 in range(0, a_ref.shape[0], SWIGLU_ROWS):
        rs = slice(r0, r0 + SWIGLU_ROWS)
        a = a_ref[rs, :]
        p = _dot(a, w1_ref[...])
        q = _dot(a, w3_ref[...])
        o_ref[rs, :] = (_silu(p) * q).astype(o_ref.dtype)


def _mm_swiglu(a, w1, w3, tm=2048, tn=256, cast_jobs=()):
    m, k = a.shape
    n = w1.shape[1]
    vm = 2 * (tm * k * 2 + 2 * k * tn * 2 + tm * tn * 2) + 4 * SWIGLU_ROWS * tn * 4
    wspec = pl.BlockSpec((k, tn), lambda i, j: (0, j))
    return _call(
        _mm_swiglu_body, (m // tm, n // tn),
        [pl.BlockSpec((tm, k), lambda i, j: (i, 0)), wspec, wspec],
        (a, w1, w3), pl.BlockSpec((tm, tn), lambda i, j: (i, j)), jax.ShapeDtypeStruct((m, n), BF16),
        vm, "matmul_swiglu", cast_jobs)


def _mm_resid_body(*refs, n_a, has_bias):
    a_refs = refs[:n_a]
    w_ref = refs[n_a]
    b_ref = refs[n_a + 1] if has_bias else None
    g_ref, x_ref, o_ref = refs[n_a + has_bias + 1:]
    a_vals = [r[...] for r in a_refs]
    for c0 in range(0, o_ref.shape[1], MXU_N):
        cs = slice(c0, c0 + MXU_N)
        k0 = 0
        acc = None
        for a in a_vals:
            part = _dot(a, w_ref[k0:k0 + a.shape[1], cs])
            acc = part if acc is None else acc + part
            k0 += a.shape[1]
        if has_bias:
            acc = acc + b_ref[:, cs]
        o_ref[:, cs] = x_ref[:, cs] + g_ref[:, cs] * acc


def _mm_resid(a_list, w, gate, xres, bias=None, tm=1024, tn=512, w_resident=False, cast_jobs=()):
    m = a_list[0].shape[0]
    k, n = w.shape
    vm = 2 * (tm * k * 2 + 2 * tm * tn * 4) + (1 if w_resident else 2) * k * tn * 2 + 2 * tm * MXU_N * 4
    if w_resident:
        grid = (n // tn, m // tm)
        ij = lambda f: (lambda j, i: f(i, j))
        wspec = pl.BlockSpec((k, tn), ij(lambda i, j: (0, j)), pipeline_mode=pl.Buffered(1))
    else:
        grid = (m // tm, n // tn)
        ij = lambda f: f
        wspec = pl.BlockSpec((k, tn), lambda i, j: (0, j))
    rowspec = pl.BlockSpec((1, tn), ij(lambda i, j: (0, j)))
    tile = pl.BlockSpec((tm, tn), ij(lambda i, j: (i, j)))
    in_specs = [pl.BlockSpec((tm, a.shape[1]), ij(lambda i, j: (i, 0))) for a in a_list] + [wspec]
    args = list(a_list) + [w]
    if bias is not None:
        in_specs.append(rowspec)
        args.append(bias)
    in_specs += [rowspec, tile]
    args += [gate, xres]
    body = functools.partial(_mm_resid_body, n_a=len(a_list), has_bias=bias is not None)
    return _call(body, grid, in_specs, args, tile, jax.ShapeDtypeStruct((m, n), F32), vm,
                 "matmul_resid_wres" if w_resident else "matmul_resid", cast_jobs)


def _ret_kernel(*refs, backward):
    if backward:
        ld_ref, q_ref, k_ref, v_ref, kc_ref, vc_ref, of_ref, g_ref, o_ref, s_ref = refs
    else:
        ld_ref, q_ref, k_ref, v_ref, kc_ref, vc_ref, o_ref, s_ref = refs
    c = RET_CHUNK
    hd = HEAD_DIM
    t = pl.program_id(1)
    ri = lax.broadcasted_iota(jnp.int32, (c, c), 0)
    ci = lax.broadcasted_iota(jnp.int32, (c, c), 1)
    dist = ((ci - ri) if backward else (ri - ci)).astype(F32)
    pos = lax.broadcasted_iota(jnp.int32, (c, 1), 0).astype(F32)
    tdims = (((0,), (0,)), ((), ()))
    n_chunks = q_ref.shape[0] // c
    order = range(n_chunks - 1, -1, -1) if backward else range(n_chunks)

    heads = []
    for hh in range(RET_HEADS):
        lg = -jnp.exp(ld_ref[hh, 0:1, 0:1])
        dmat = jnp.where(dist >= 0, jnp.exp(lg * jnp.maximum(dist, 0.0)), 0.0)
        if backward:
            qdec = jnp.exp(lg * (c - pos))
            kdec = jnp.exp(lg * pos)
        else:
            qdec = jnp.exp(lg * (pos + 1.0))
            kdec = jnp.exp(lg * (c - 1.0 - pos))
        cdec = jnp.exp(lg * float(c))
        heads.append((lg, dmat, qdec, kdec, cdec))

    @pl.when(t == 0)
    def _():
        lc = kc_ref.shape[0]
        jpos = lax.broadcasted_iota(jnp.int32, (lc, 1), 0).astype(F32)
        for hh in range(RET_HEADS):
            cs = slice(hh * hd, (hh + 1) * hd)
            w = jnp.exp(heads[hh][0] * (jpos if backward else (lc - 1.0 - jpos)))
            kcw = ((kc_ref[:, cs] * QK_SCALE) * w).astype(BF16)
            s_ref[hh] = lax.dot_general(kcw, vc_ref[:, cs].astype(BF16), tdims, preferred_element_type=F32)

    for cidx in order:
        rows = pl.ds(cidx * c, c)
        for hh in range(RET_HEADS):
            _, dmat, qdec, kdec, cdec = heads[hh]
            cs = slice(hh * hd, (hh + 1) * hd)
            q = q_ref[rows, cs]
            k = k_ref[rows, cs]
            v = v_ref[rows, cs]
            s = s_ref[hh]
            scores = lax.dot_general(q, k, (((1,), (1,)), ((), ())), preferred_element_type=F32) * dmat
            intra = jnp.dot(scores.astype(BF16), v, preferred_element_type=F32)
            q_in = (q.astype(F32) * qdec).astype(BF16)
            cross = jnp.dot(q_in, s.astype(BF16), preferred_element_type=F32)
            k_in = (k.astype(F32) * kdec).astype(BF16)
            s_ref[hh] = cdec * s + lax.dot_general(k_in, v, tdims, preferred_element_type=F32)
            o = intra + cross
            if backward:
                y = of_ref[rows, cs] + o
                mu = jnp.mean(y, axis=-1, keepdims=True)
                yc = y - mu
                yn = yc * lax.rsqrt(jnp.mean(yc * yc, axis=-1, keepdims=True) + EPS)
                o_ref[rows, cs] = (_silu(g_ref[rows, cs]) * yn).astype(o_ref.dtype)
            else:
                o_ref[rows, cs] = o


def _retention(ld, qkv, kvc, ofwd=None, rest=None):
    backward = ofwd is not None
    seq = qkv.shape[0]
    nt = seq // RET_ROWS
    tt = (lambda t: nt - 1 - t) if backward else (lambda t: t)
    dirn = 1 if backward else 0
    hd = HEAD_DIM
    wd = RET_HEADS * hd
    ng = N_HEADS // RET_HEADS
    blk = lambda off: pl.BlockSpec((RET_ROWS, wd), lambda h, t: (tt(t), h + off))
    in_specs = [
        pl.BlockSpec((None, RET_HEADS, 8, LANES), lambda h, t: (dirn, h, 0, 0)),
        blk(0), blk(ng), blk(2 * ng),
        pl.BlockSpec((CTX_LEN, wd), lambda h, t: (0, h)),
        pl.BlockSpec((CTX_LEN, wd), lambda h, t: (0, h + ng)),
    ]
    args = [ld, qkv, qkv, qkv, kvc, kvc]
    if backward:
        in_specs += [blk(0), blk(0)]
        args += [ofwd, rest]
    out_dtype = BF16 if backward else F32
    tile = RET_ROWS * wd
    vm = 2 * (3 * tile * 2 + 2 * CTX_LEN * wd * 4 + (2 * tile * 4 if backward else 0)
              + tile * jnp.dtype(out_dtype).itemsize) + 8 * RET_HEADS * RET_CHUNK * RET_CHUNK * 4
    return pl.pallas_call(
        functools.partial(_ret_kernel, backward=backward),
        grid=(ng, nt),
        in_specs=in_specs,
        out_specs=blk(0),
        out_shape=jax.ShapeDtypeStruct((seq, D_RET), out_dtype),
        scratch_shapes=[pltpu.VMEM((RET_HEADS, hd, hd), F32)],
        compiler_params=_params(("arbitrary", "arbitrary"), vm),
        name="retention_bwd" if backward else "retention_fwd",
    )(*args)


def _sconv_kernel(bg_ref, cg_ref, xin_ref, cgp_ref, xinp_ref, cgn_ref, xinn_ref, w_ref, o_ref, p_ref):
    i = pl.program_id(0)
    tr = cg_ref.shape[0]
    p_ref[8:8 + tr, :] = cg_ref[...] * xin_ref[...]
    p_ref[0:8, :] = jnp.where(i > 0, cgp_ref[...] * xinp_ref[...], 0.0)
    p_ref[8 + tr:16 + tr, :] = jnp.where(i < pl.num_programs(0) - 1, cgn_ref[...] * xinn_ref[...], 0.0)
    conv = (w_ref[0:1, :] * p_ref[7:7 + tr, :] + w_ref[1:2, :] * p_ref[8:8 + tr, :]
            + w_ref[2:3, :] * p_ref[9:9 + tr, :])
    o_ref[...] = (bg_ref[...] * conv).astype(o_ref.dtype)


def _short_conv(rest, sc_w, tr=512, tc=1024):
    seq = rest.shape[0]
    nct = D_SC // tc
    r8 = tr // 8
    last8 = seq // 8 - 1
    main = lambda part: pl.BlockSpec((tr, tc), lambda i, j: (i, j + part * nct))
    prev = lambda part: pl.BlockSpec((8, tc), lambda i, j: (jnp.maximum(i * r8 - 1, 0), j + part * nct))
    nxt = lambda part: pl.BlockSpec((8, tc), lambda i, j: (jnp.minimum((i + 1) * r8, last8), j + part * nct))
    return pl.pallas_call(
        _sconv_kernel,
        grid=(seq // tr, nct),
        in_specs=[main(1), main(2), main(3), prev(2), prev(3), nxt(2), nxt(3),
                  pl.BlockSpec((SC_WIDTH, tc), lambda i, j: (0, j))],
        out_specs=pl.BlockSpec((tr, tc), lambda i, j: (i, j)),
        out_shape=jax.ShapeDtypeStruct((seq, D_SC), BF16),
        scratch_shapes=[pltpu.VMEM((tr + 16, tc), F32)],
        compiler_params=_params(("arbitrary", "arbitrary"), 10 * tr * tc * 4),
        name="short_conv",
    )(rest, rest, rest, rest, rest, rest, rest, sc_w)


CONF_HALO = 16
CONF_ROW_CHUNK = 64


def _dwconv_kernel(u_ref, up_ref, un_ref, w_ref, b_ref, g_ref, beta_ref, o_ref, p_ref, c_ref):
    i = pl.program_id(0)
    tr, d = u_ref.shape
    hl = CONF_HALO
    pad = (CONF_KERNEL - 1) // 2
    p_ref[hl:hl + tr, :] = u_ref[...]
    p_ref[0:hl, :] = jnp.where(i > 0, up_ref[...], 0.0)
    p_ref[hl + tr:2 * hl + tr, :] = jnp.where(i < pl.num_programs(0) - 1, un_ref[...], 0.0)
    rc = CONF_ROW_CHUNK

    off = hl - pad

    def strip_body(s, carry):
        cols = pl.ds(pl.multiple_of(s * LANES, LANES), LANES)
        wv = w_ref[:, cols]
        bias = b_ref[:, cols]
        for r0 in range(0, tr, rc):
            out = None
            for e in range(8):
                q = None
                for j in range(CONF_KERNEL):
                    if (off + j) % 8 != e:
                        continue
                    a8 = r0 + (off + j) - e
                    term = wv[j:j + 1, :] * p_ref[a8:a8 + rc + 8, cols]
                    q = term if q is None else q + term
                part = q[e:e + rc, :]
                out = part if out is None else out + part
            c_ref[r0:r0 + rc, cols] = out + bias
        return carry

    lax.fori_loop(0, d // LANES, strip_body, 0)
    x = c_ref[...]
    mu = jnp.mean(x, axis=-1, keepdims=True)
    xc = x - mu
    y = xc * lax.rsqrt(jnp.mean(xc * xc, axis=-1, keepdims=True) + EPS)
    y = y * g_ref[...] + beta_ref[...]
    o_ref[...] = _silu(y).astype(o_ref.dtype)


def _dwconv_ln_silu(u, dw_w, dw_b, ln_g, ln_b, tr=256):
    seq, d = u.shape
    hl = CONF_HALO
    rh = tr // hl
    lasth = seq // hl - 1
    row = pl.BlockSpec((1, d), lambda i: (0, 0))
    return pl.pallas_call(
        _dwconv_kernel,
        grid=(seq // tr,),
        in_specs=[pl.BlockSpec((tr, d), lambda i: (i, 0)),
                  pl.BlockSpec((hl, d), lambda i: (jnp.maximum(i * rh - 1, 0), 0)),
                  pl.BlockSpec((hl, d), lambda i: (jnp.minimum((i + 1) * rh, lasth), 0)),
                  pl.BlockSpec((CONF_KERNEL, d), lambda i: (0, 0)),
                  row, row, row],
        out_specs=pl.BlockSpec((tr, d), lambda i: (i, 0)),
        out_shape=jax.ShapeDtypeStruct((seq, d), BF16),
        scratch_shapes=[pltpu.VMEM((tr + 2 * hl, d), F32), pltpu.VMEM((tr, d), F32)],
        compiler_params=_params(("arbitrary",), 8 * tr * d * 4),
        name="dwconv_ln_silu",
    )(u, u, u, dw_w, dw_b, ln_g, ln_b)


def _rope_tables(seq):
    n_rows = seq // GRID_W
    rows = jnp.broadcast_to(jnp.arange(n_rows, dtype=F32)[:, None], (n_rows, GRID_W)).reshape(-1)
    cols = jnp.broadcast_to(jnp.arange(GRID_W, dtype=F32)[None, :], (n_rows, GRID_W)).reshape(-1)
    n_freq = HEAD_DIM // 4
    freqs = ROPE_BASE ** (-jnp.arange(n_freq, dtype=F32) / n_freq)
    ang = jnp.concatenate([rows[:, None] * freqs, cols[:, None] * freqs], axis=-1)
    return jnp.cos(ang), jnp.sin(ang)


def kernel(x, c, ctx, c_ctx, ada_w, ada_b, norm_mix, norm_ffn, norm_final, even_w_in, ret_log_decay, sc_conv_w, even_w_out, conf_w_pw1, conf_b_pw1, conf_dw_w, conf_dw_b, conf_ln_g, conf_ln_b, conf_w_pw2, conf_b_pw2, ffn_w1, ffn_w3, ffn_w2):
    d = D_MODEL
    seq = x.shape[1]
    xs = x[0]
    row = lambda v: v.reshape(1, -1)

    cc8 = jnp.zeros((8, d), F32).at[0].set(c[0]).at[1].set(c_ctx)
    mods = _ada(cc8, ada_w, ada_b)
    cos, sin = _rope_tables(seq)
    mod_rows = lambda i: [row(mods[i, 0][n * d:(n + 1) * d]) for n in range(N_MOD)]

    sh1, sc1, g1, sh2, sc2, g2 = mod_rows(0)
    h = _normmod(xs, row(norm_mix[0]), sh1, sc1)
    mc = mods[0, 1]
    hc = _normmod(ctx[0], row(norm_mix[0]), row(mc[:d]), row(mc[d:2 * d]))
    w_in = even_w_in[0].astype(BF16)
    qkv, w_out, w1_0 = _mm_rope(h, w_in, cos, sin, cast_jobs=((even_w_out, 0), (ffn_w1, 0)))
    rest, w3_0, w2_0 = _mm_plain(h, w_in, 3 * D_RET, D_RET + 3 * D_SC, F32, 1024, 1024,
                                 cast_jobs=((ffn_w3, 0), (ffn_w2, 0)))
    kvc = _mm_plain(hc, w_in, D_RET, 2 * D_RET, F32, CTX_LEN, 1024)
    ld = jnp.broadcast_to(ret_log_decay[0][:, :, None, None], (2, N_HEADS, 8, LANES))
    ofwd = _retention(ld, qkv, kvc)
    ret = _retention(ld, qkv, kvc, ofwd, rest)
    sc = _short_conv(rest, sc_conv_w[0])
    xs = _mm_resid([ret, sc], w_out, g1, xs, tn=1024)
    h = _normmod(xs, row(norm_ffn[0]), sh2, sc2)
    u, w_pw1, w_pw2 = _mm_swiglu(h, w1_0, w3_0, cast_jobs=((conf_w_pw1, 0), (conf_w_pw2, 0)))
    xs = _mm_resid([u], w2_0, g2, xs, tm=512, tn=1024, w_resident=True)

    sh1, sc1, g1, sh2, sc2, g2 = mod_rows(1)
    h = _normmod(xs, row(norm_mix[1]), sh1, sc1)
    u, w1_1, w3_1 = _mm_glu(h, w_pw1, row(conf_b_pw1[0]), d, F32, cast_jobs=((ffn_w1, 1), (ffn_w3, 1)))
    v = _dwconv_ln_silu(u, conf_dw_w[0], row(conf_dw_b[0]), row(conf_ln_g[0]), row(conf_ln_b[0]))
    xs = _mm_resid([v], w_pw2, g1, xs, bias=row(conf_b_pw2[0]), tn=1024)
    h = _normmod(xs, row(norm_ffn[1]), sh2, sc2)
    u, w2_1 = _mm_swiglu(h, w1_1, w3_1, cast_jobs=((ffn_w2, 1),))
    xs = _mm_resid([u], w2_1, g2, xs, tm=512, tn=1024, w_resident=True)
    out = _rmsnorm(xs, row(norm_final))
    return out[None]
```

```python
import functools

import jax
import jax.numpy as jnp
from jax import lax
from jax.experimental import pallas as pl
from jax.experimental.pallas import tpu as pltpu

F32 = jnp.float32
BF16 = jnp.bfloat16

D_MODEL = 4096
GRID_W = 64
CTX_LEN = 256
D_RET = D_MODEL // 2
HEAD_DIM = 256
N_HEADS = D_RET // HEAD_DIM
D_SC = D_MODEL - D_RET
SC_WIDTH = 3
ROPE_BASE = 10000.0
CONF_KERNEL = 31
N_MOD = 6
EPS = 1e-6
QK_SCALE = HEAD_DIM ** -0.5

V7X_VMEM_BYTES = 64 * 1024 * 1024
VMEM_CEILING = V7X_VMEM_BYTES - 6 * 1024 * 1024
LANES = 128
RET_CHUNK = 256
RET_ROWS = 1024
RET_HEADS = 4


def _params(sem, vmem_bytes):
    limit = min(int(vmem_bytes * 1.2) + (4 << 20), VMEM_CEILING)
    return pltpu.CompilerParams(dimension_semantics=sem, vmem_limit_bytes=limit)


def _silu(v):
    return v * jax.nn.sigmoid(v)


def _ada_kernel(cc_ref, w_ref, b_ref, o_ref):
    s = _silu(cc_ref[...]).astype(BF16)
    acc = jnp.dot(s, w_ref[...].astype(BF16), preferred_element_type=F32)
    o_ref[...] = acc + b_ref[...]


def _ada(cc8, ada_w, ada_b):
    depth, d, n = ada_w.shape
    tn = 512
    return pl.pallas_call(
        _ada_kernel,
        grid=(depth, n // tn),
        in_specs=[
            pl.BlockSpec((8, d), lambda l, j: (0, 0)),
            pl.BlockSpec((None, d, tn), lambda l, j: (l, 0, j)),
            pl.BlockSpec((None, 1, tn), lambda l, j: (l, 0, j)),
        ],
        out_specs=pl.BlockSpec((None, 8, tn), lambda l, j: (l, 0, j)),
        out_shape=jax.ShapeDtypeStruct((depth, 8, n), F32),
        compiler_params=_params(("arbitrary", "arbitrary"), 3 * d * tn * 4),
        name="ada_mod",
    )(cc8, ada_w, ada_b.reshape(depth, 1, n))


def _normmod_kernel(x_ref, g_ref, sh_ref, sc_ref, o_ref):
    x = x_ref[...]
    r = lax.rsqrt(jnp.mean(x * x, axis=-1, keepdims=True) + EPS)
    y = (x * r) * g_ref[...]
    o_ref[...] = (y * (1.0 + sc_ref[...]) + sh_ref[...]).astype(o_ref.dtype)


def _rmsnorm_kernel(x_ref, g_ref, o_ref):
    x = x_ref[...]
    r = lax.rsqrt(jnp.mean(x * x, axis=-1, keepdims=True) + EPS)
    o_ref[...] = ((x * r) * g_ref[...]).astype(o_ref.dtype)


def _normmod(x, g, shift, scale, tr=512):
    m, d = x.shape
    tr = min(tr, m)
    row = pl.BlockSpec((1, d), lambda i: (0, 0))
    return pl.pallas_call(
        _normmod_kernel,
        grid=(m // tr,),
        in_specs=[pl.BlockSpec((tr, d), lambda i: (i, 0)), row, row, row],
        out_specs=pl.BlockSpec((tr, d), lambda i: (i, 0)),
        out_shape=jax.ShapeDtypeStruct((m, d), BF16),
        compiler_params=_params(("parallel",), 6 * tr * d * 4),
        name="norm_modulate",
    )(x, g, shift, scale)


def _rmsnorm(x, g, tr=512):
    m, d = x.shape
    return pl.pallas_call(
        _rmsnorm_kernel,
        grid=(m // tr,),
        in_specs=[pl.BlockSpec((tr, d), lambda i: (i, 0)), pl.BlockSpec((1, d), lambda i: (0, 0))],
        out_specs=pl.BlockSpec((tr, d), lambda i: (i, 0)),
        out_shape=jax.ShapeDtypeStruct((m, d), F32),
        compiler_params=_params(("parallel",), 6 * tr * d * 4),
        name="final_norm",
    )(x, g)


MXU_N = 256


def _dot(a, b):
    return jnp.dot(a, b, preferred_element_type=F32)


def _call(body, grid, in_specs, args, out_spec, out_shape, vmem, name, cast_jobs=()):
    steps = 1
    for g in grid:
        steps *= g
    job_in, job_out, job_shapes, job_args = [], [], [], []
    for src, layer in cast_jobs:
        r, c = src.shape[-2:]
        br = next(b for b in (16, 32, 64, 128, 256, 512, 1024, 2048, 4096) if r % b == 0 and r // b <= steps)
        nblk = r // br

        def blk(*g, nblk=nblk):
            s = g[0]
            for gi, n in zip(g[1:], grid[1:]):
                s = s * n + gi
            return jnp.minimum(s, nblk - 1)

        if layer is None:
            job_in.append(pl.BlockSpec((br, c), lambda *g, blk=blk: (blk(*g), 0)))
        else:
            job_in.append(pl.BlockSpec((None, br, c), lambda *g, blk=blk, layer=layer: (layer, blk(*g), 0)))
        job_out.append(pl.BlockSpec((br, c), lambda *g, blk=blk: (blk(*g), 0)))
        job_shapes.append(jax.ShapeDtypeStruct((r, c), BF16))
        job_args.append(src)
        vmem += 2 * br * c * (4 + 2)
    n_in, n_jobs = len(in_specs), len(cast_jobs)

    def kern(*refs):
        ins = refs[:n_in]
        jin = refs[n_in:n_in + n_jobs]
        out = refs[n_in + n_jobs]
        jout = refs[n_in + n_jobs + 1:]
        for s_ref, d_ref in zip(jin, jout):
            d_ref[...] = s_ref[...].astype(d_ref.dtype)
        body(*ins, out)

    res = pl.pallas_call(
        kern,
        grid=grid,
        in_specs=list(in_specs) + job_in,
        out_specs=[out_spec] + job_out,
        out_shape=[out_shape] + job_shapes,
        compiler_params=_params(("arbitrary",) * len(grid), vmem),
        name=name,
    )(*args, *job_args)
    return res if cast_jobs else res[0]


def _mm_plain_body(a_ref, w_ref, o_ref):
    a = a_ref[...]
    for c0 in range(0, o_ref.shape[1], MXU_N):
        o_ref[:, c0:c0 + MXU_N] = _dot(a, w_ref[:, c0:c0 + MXU_N]).astype(o_ref.dtype)


def _mm_plain(a, w, col0, n, out_dtype, tm, tn, cast_jobs=()):
    m, k = a.shape
    tm = min(tm, m)
    jb = col0 // tn
    osz = jnp.dtype(out_dtype).itemsize
    vm = 2 * (tm * k * 2 + k * tn * 2 + tm * tn * osz) + 2 * tm * MXU_N * 4
    return _call(
        _mm_plain_body, (m // tm, n // tn),
        [pl.BlockSpec((tm, k), lambda i, j: (i, 0)), pl.BlockSpec((k, tn), lambda i, j: (0, j + jb))],
        (a, w), pl.BlockSpec((tm, tn), lambda i, j: (i, j)), jax.ShapeDtypeStruct((m, n), out_dtype),
        vm, "matmul_plain", cast_jobs)


def _mm_rope_body(a_ref, w_ref, cos_ref, sin_ref, o_ref, *, n_rope_tiles, n_q_tiles):
    j = pl.program_id(1)
    tn = o_ref.shape[1]
    half = HEAD_DIM // 2

    @pl.when(j < n_rope_tiles)
    def _():
        a = a_ref[...]
        cos = cos_ref[...]
        sin = sin_ref[...]
        scale = jnp.where(j >= n_q_tiles, QK_SCALE, 1.0).astype(F32)
        for lo in range(0, tn, HEAD_DIM):
            acc = _dot(a, w_ref[:, lo:lo + HEAD_DIM])
            t1 = acc[:, :half]
            t2 = acc[:, half:]
            o_ref[:, lo:lo + half] = ((t1 * cos - t2 * sin) * scale).astype(o_ref.dtype)
            o_ref[:, lo + half:lo + HEAD_DIM] = ((t1 * sin + t2 * cos) * scale).astype(o_ref.dtype)

    @pl.when(j >= n_rope_tiles)
    def _():
        _mm_plain_body(a_ref, w_ref, o_ref)


def _mm_rope(h, w, cos, sin, tm=1024, tn=1024, cast_jobs=()):
    m, k = h.shape
    n = 3 * D_RET
    body = functools.partial(_mm_rope_body, n_rope_tiles=2 * D_RET // tn, n_q_tiles=D_RET // tn)
    vm = 2 * (tm * k * 2 + k * tn * 2 + tm * tn * 2 + 2 * tm * LANES * 4) + 4 * tm * MXU_N * 4
    tab = pl.BlockSpec((tm, HEAD_DIM // 2), lambda i, j: (i, 0))
    return _call(
        body, (m // tm, n // tn),
        [pl.BlockSpec((tm, k), lambda i, j: (i, 0)), pl.BlockSpec((k, tn), lambda i, j: (0, j)), tab, tab],
        (h, w, cos, sin), pl.BlockSpec((tm, tn), lambda i, j: (i, j)), jax.ShapeDtypeStruct((m, n), BF16),
        vm, "matmul_qkv_rope", cast_jobs)


def _mm_glu_body(a_ref, wa_ref, wg_ref, ba_ref, bg_ref, o_ref):
    a = a_ref[...]
    for c0 in range(0, o_ref.shape[1], MXU_N):
        cs = slice(c0, c0 + MXU_N)
        p = _dot(a, wa_ref[:, cs]) + ba_ref[:, cs]
        q = _dot(a, wg_ref[:, cs]) + bg_ref[:, cs]
        o_ref[:, cs] = (p * jax.nn.sigmoid(q)).astype(o_ref.dtype)


def _mm_glu(a, w, b, n, out_dtype, tm=1024, tn=512, cast_jobs=()):
    m, k = a.shape
    jb = n // tn
    osz = jnp.dtype(out_dtype).itemsize
    vm = 2 * (tm * k * 2 + 2 * k * tn * 2 + tm * tn * osz) + 4 * tm * MXU_N * 4
    return _call(
        _mm_glu_body, (m // tm, n // tn),
        [pl.BlockSpec((tm, k), lambda i, j: (i, 0)),
         pl.BlockSpec((k, tn), lambda i, j: (0, j)),
         pl.BlockSpec((k, tn), lambda i, j: (0, j + jb)),
         pl.BlockSpec((1, tn), lambda i, j: (0, j)),
         pl.BlockSpec((1, tn), lambda i, j: (0, j + jb))],
        (a, w, w, b, b), pl.BlockSpec((tm, tn), lambda i, j: (i, j)), jax.ShapeDtypeStruct((m, n), out_dtype),
        vm, "matmul_glu", cast_jobs)


SWIGLU_ROWS = 1024


def _mm_swiglu_body(a_ref, w1_ref, w3_ref, o_ref):
    for r0 in range(0, a_ref.shape[0], SWIGLU_ROWS):
        rs = slice(r0, r0 + SWIGLU_ROWS)
        a = a_ref[rs, :]
        p = _dot(a, w1_ref[...])
        q = _dot(a, w3_ref[...])
        o_ref[rs, :] = (_silu(p) * q).astype(o_ref.dtype)


def _mm_swiglu(a, w1, w3, tm=2048, tn=256, cast_jobs=()):
    m, k = a.shape
    n = w1.shape[1]
    vm = 2 * (tm * k * 2 + 2 * k * tn * 2 + tm * tn * 2) + 4 * SWIGLU_ROWS * tn * 4
    wspec = pl.BlockSpec((k, tn), lambda i, j: (0, j))
    return _call(
        _mm_swiglu_body, (m // tm, n // tn),
        [pl.BlockSpec((tm, k), lambda i, j: (i, 0)), wspec, wspec],
        (a, w1, w3), pl.BlockSpec((tm, tn), lambda i, j: (i, j)), jax.ShapeDtypeStruct((m, n), BF16),
        vm, "matmul_swiglu", cast_jobs)


def _mm_resid_body(*refs, n_a, has_bias):
    a_refs = refs[:n_a]
    w_ref = refs[n_a]
    b_ref = refs[n_a + 1] if has_bias else None
    g_ref, x_ref, o_ref = refs[n_a + has_bias + 1:]
    a_vals = [r[...] for r in a_refs]
    for c0 in range(0, o_ref.shape[1], MXU_N):
        cs = slice(c0, c0 + MXU_N)
        k0 = 0
        acc = None
        for a in a_vals:
            part = _dot(a, w_ref[k0:k0 + a.shape[1], cs])
            acc = part if acc is None else acc + part
            k0 += a.shape[1]
        if has_bias:
            acc = acc + b_ref[:, cs]
        o_ref[:, cs] = x_ref[:, cs] + g_ref[:, cs] * acc


def _mm_resid(a_list, w, gate, xres, bias=None, tm=1024, tn=512, w_resident=False, cast_jobs=()):
    m = a_list[0].shape[0]
    k, n = w.shape
    vm = 2 * (tm * k * 2 + 2 * tm * tn * 4) + (1 if w_resident else 2) * k * tn * 2 + 2 * tm * MXU_N * 4
    if w_resident:
        grid = (n // tn, m // tm)
        ij = lambda f: (lambda j, i: f(i, j))
        wspec = pl.BlockSpec((k, tn), ij(lambda i, j: (0, j)), pipeline_mode=pl.Buffered(1))
    else:
        grid = (m // tm, n // tn)
        ij = lambda f: f
        wspec = pl.BlockSpec((k, tn), lambda i, j: (0, j))
    rowspec = pl.BlockSpec((1, tn), ij(lambda i, j: (0, j)))
    tile = pl.BlockSpec((tm, tn), ij(lambda i, j: (i, j)))
    in_specs = [pl.BlockSpec((tm, a.shape[1]), ij(lambda i, j: (i, 0))) for a in a_list] + [wspec]
    args = list(a_list) + [w]
    if bias is not None:
        in_specs.append(rowspec)
        args.append(bias)
    in_specs += [rowspec, tile]
    args += [gate, xres]
    body = functools.partial(_mm_resid_body, n_a=len(a_list), has_bias=bias is not None)
    return _call(body, grid, in_specs, args, tile, jax.ShapeDtypeStruct((m, n), F32), vm,
                 "matmul_resid_wres" if w_resident else "matmul_resid", cast_jobs)


def _ret_kernel(*refs, backward):
    if backward:
        ld_ref, q_ref, k_ref, v_ref, kc_ref, vc_ref, of_ref, g_ref, o_ref, s_ref = refs
    else:
        ld_ref, q_ref, k_ref, v_ref, kc_ref, vc_ref, o_ref, s_ref = refs
    c = RET_CHUNK
    hd = HEAD_DIM
    t = pl.program_id(1)
    ri = lax.broadcasted_iota(jnp.int32, (c, c), 0)
    ci = lax.broadcasted_iota(jnp.int32, (c, c), 1)
    dist = ((ci - ri) if backward else (ri - ci)).astype(F32)
    pos = lax.broadcasted_iota(jnp.int32, (c, 1), 0).astype(F32)
    tdims = (((0,), (0,)), ((), ()))
    n_chunks = q_ref.shape[0] // c
    order = range(n_chunks - 1, -1, -1) if backward else range(n_chunks)

    heads = []
    for hh in range(RET_HEADS):
        lg = -jnp.exp(ld_ref[hh, 0:1, 0:1])
        dmat = jnp.where(dist >= 0, jnp.exp(lg * jnp.maximum(dist, 0.0)), 0.0)
        if backward:
            qdec = jnp.exp(lg * (c - pos))
            kdec = jnp.exp(lg * pos)
        else:
            qdec = jnp.exp(lg * (pos + 1.0))
            kdec = jnp.exp(lg * (c - 1.0 - pos))
        cdec = jnp.exp(lg * float(c))
        heads.append((lg, dmat, qdec, kdec, cdec))

    @pl.when(t == 0)
    def _():
        lc = kc_ref.shape[0]
        jpos = lax.broadcasted_iota(jnp.int32, (lc, 1), 0).astype(F32)
        for hh in range(RET_HEADS):
            cs = slice(hh * hd, (hh + 1) * hd)
            w = jnp.exp(heads[hh][0] * (jpos if backward else (lc - 1.0 - jpos)))
            kcw = ((kc_ref[:, cs] * QK_SCALE) * w).astype(BF16)
            s_ref[hh] = lax.dot_general(kcw, vc_ref[:, cs].astype(BF16), tdims, preferred_element_type=F32)

    for cidx in order:
        rows = pl.ds(cidx * c, c)
        for hh in range(RET_HEADS):
            _, dmat, qdec, kdec, cdec = heads[hh]
            cs = slice(hh * hd, (hh + 1) * hd)
            q = q_ref[rows, cs]
            k = k_ref[rows, cs]
            v = v_ref[rows, cs]
            s = s_ref[hh]
            scores = lax.dot_general(q, k, (((1,), (1,)), ((), ())), preferred_element_type=F32) * dmat
            intra = jnp.dot(scores.astype(BF16), v, preferred_element_type=F32)
            q_in = (q.astype(F32) * qdec).astype(BF16)
            cross = jnp.dot(q_in, s.astype(BF16), preferred_element_type=F32)
            k_in = (k.astype(F32) * kdec).astype(BF16)
            s_ref[hh] = cdec * s + lax.dot_general(k_in, v, tdims, preferred_element_type=F32)
            o = intra + cross
            if backward:
                y = of_ref[rows, cs] + o
                mu = jnp.mean(y, axis=-1, keepdims=True)
                yc = y - mu
                yn = yc * lax.rsqrt(jnp.mean(yc * yc, axis=-1, keepdims=True) + EPS)
                o_ref[rows, cs] = (_silu(g_ref[rows, cs]) * yn).astype(o_ref.dtype)
            else:
                o_ref[rows, cs] = o


def _retention(ld, qkv, kvc, ofwd=None, rest=None):
    backward = ofwd is not None
    seq = qkv.shape[0]
    nt = seq // RET_ROWS
    tt = (lambda t: nt - 1 - t) if backward else (lambda t: t)
    dirn = 1 if backward else 0
    hd = HEAD_DIM
    wd = RET_HEADS * hd
    ng = N_HEADS // RET_HEADS
    blk = lambda off: pl.BlockSpec((RET_ROWS, wd), lambda h, t: (tt(t), h + off))
    in_specs = [
        pl.BlockSpec((None, RET_HEADS, 8, LANES), lambda h, t: (dirn, h, 0, 0)),
        blk(0), blk(ng), blk(2 * ng),
        pl.BlockSpec((CTX_LEN, wd), lambda h, t: (0, h)),
        pl.BlockSpec((CTX_LEN, wd), lambda h, t: (0, h + ng)),
    ]
    args = [ld, qkv, qkv, qkv, kvc, kvc]
    if backward:
        in_specs += [blk(0), blk(0)]
        args += [ofwd, rest]
    out_dtype = BF16 if backward else F32
    tile = RET_ROWS * wd
    vm = 2 * (3 * tile * 2 + 2 * CTX_LEN * wd * 4 + (2 * tile * 4 if backward else 0)
              + tile * jnp.dtype(out_dtype).itemsize) + 8 * RET_HEADS * RET_CHUNK * RET_CHUNK * 4
    return pl.pallas_call(
        functools.partial(_ret_kernel, backward=backward),
        grid=(ng, nt),
        in_specs=in_specs,
        out_specs=blk(0),
        out_shape=jax.ShapeDtypeStruct((seq, D_RET), out_dtype),
        scratch_shapes=[pltpu.VMEM((RET_HEADS, hd, hd), F32)],
        compiler_params=_params(("arbitrary", "arbitrary"), vm),
        name="retention_bwd" if backward else "retention_fwd",
    )(*args)


def _sconv_kernel(bg_ref, cg_ref, xin_ref, cgp_ref, xinp_ref, cgn_ref, xinn_ref, w_ref, o_ref, p_ref):
    i = pl.program_id(0)
    tr = cg_ref.shape[0]
    p_ref[8:8 + tr, :] = cg_ref[...] * xin_ref[...]
    p_ref[0:8, :] = jnp.where(i > 0, cgp_ref[...] * xinp_ref[...], 0.0)
    p_ref[8 + tr:16 + tr, :] = jnp.where(i < pl.num_programs(0) - 1, cgn_ref[...] * xinn_ref[...], 0.0)
    conv = (w_ref[0:1, :] * p_ref[7:7 + tr, :] + w_ref[1:2, :] * p_ref[8:8 + tr, :]
            + w_ref[2:3, :] * p_ref[9:9 + tr, :])
    o_ref[...] = (bg_ref[...] * conv).astype(o_ref.dtype)


def _short_conv(rest, sc_w, tr=512, tc=1024):
    seq = rest.shape[0]
    nct = D_SC // tc
    r8 = tr // 8
    last8 = seq // 8 - 1
    main = lambda part: pl.BlockSpec((tr, tc), lambda i, j: (i, j + part * nct))
    prev = lambda part: pl.BlockSpec((8, tc), lambda i, j: (jnp.maximum(i * r8 - 1, 0), j + part * nct))
    nxt = lambda part: pl.BlockSpec((8, tc), lambda i, j: (jnp.minimum((i + 1) * r8, last8), j + part * nct))
    return pl.pallas_call(
        _sconv_kernel,
        grid=(seq // tr, nct),
        in_specs=[main(1), main(2), main(3), prev(2), prev(3), nxt(2), nxt(3),
                  pl.BlockSpec((SC_WIDTH, tc), lambda i, j: (0, j))],
        out_specs=pl.BlockSpec((tr, tc), lambda i, j: (i, j)),
        out_shape=jax.ShapeDtypeStruct((seq, D_SC), BF16),
        scratch_shapes=[pltpu.VMEM((tr + 16, tc), F32)],
        compiler_params=_params(("arbitrary", "arbitrary"), 10 * tr * tc * 4),
        name="short_conv",
    )(rest, rest, rest, rest, rest, rest, rest, sc_w)


CONF_HALO = 16
CONF_ROW_CHUNK = 64


def _dwconv_kernel(u_ref, up_ref, un_ref, w_ref, b_ref, g_ref, beta_ref, o_ref, p_ref, c_ref):
    i = pl.program_id(0)
    tr, d = u_ref.shape
    hl = CONF_HALO
    pad = (CONF_KERNEL - 1) // 2
    p_ref[hl:hl + tr, :] = u_ref[...]
    p_ref[0:hl, :] = jnp.where(i > 0, up_ref[...], 0.0)
    p_ref[hl + tr:2 * hl + tr, :] = jnp.where(i < pl.num_programs(0) - 1, un_ref[...], 0.0)
    rc = CONF_ROW_CHUNK

    off = hl - pad

    def strip_body(s, carry):
        cols = pl.ds(pl.multiple_of(s * LANES, LANES), LANES)
        wv = w_ref[:, cols]
        bias = b_ref[:, cols]
        for r0 in range(0, tr, rc):
            out = None
            for e in range(8):
                q = None
                for j in range(CONF_KERNEL):
                    if (off + j) % 8 != e:
                        continue
                    a8 = r0 + (off + j) - e
                    term = wv[j:j + 1, :] * p_ref[a8:a8 + rc + 8, cols]
                    q = term if q is None else q + term
                part = q[e:e + rc, :]
                out = part if out is None else out + part
            c_ref[r0:r0 + rc, cols] = out + bias
        return carry

    lax.fori_loop(0, d // LANES, strip_body, 0)
    x = c_ref[...]
    mu = jnp.mean(x, axis=-1, keepdims=True)
    xc = x - mu
    y = xc * lax.rsqrt(jnp.mean(xc * xc, axis=-1, keepdims=True) + EPS)
    y = y * g_ref[...] + beta_ref[...]
    o_ref[...] = _silu(y).astype(o_ref.dtype)


def _dwconv_ln_silu(u, dw_w, dw_b, ln_g, ln_b, tr=256):
    seq, d = u.shape
    hl = CONF_HALO
    rh = tr // hl
    lasth = seq // hl - 1
    row = pl.BlockSpec((1, d), lambda i: (0, 0))
    return pl.pallas_call(
        _dwconv_kernel,
        grid=(seq // tr,),
        in_specs=[pl.BlockSpec((tr, d), lambda i: (i, 0)),
                  pl.BlockSpec((hl, d), lambda i: (jnp.maximum(i * rh - 1, 0), 0)),
                  pl.BlockSpec((hl, d), lambda i: (jnp.minimum((i + 1) * rh, lasth), 0)),
                  pl.BlockSpec((CONF_KERNEL, d), lambda i: (0, 0)),
                  row, row, row],
        out_specs=pl.BlockSpec((tr, d), lambda i: (i, 0)),
        out_shape=jax.ShapeDtypeStruct((seq, d), BF16),
        scratch_shapes=[pltpu.VMEM((tr + 2 * hl, d), F32), pltpu.VMEM((tr, d), F32)],
        compiler_params=_params(("arbitrary",), 8 * tr * d * 4),
        name="dwconv_ln_silu",
    )(u, u, u, dw_w, dw_b, ln_g, ln_b)


def _rope_tables(seq):
    n_rows = seq // GRID_W
    rows = jnp.broadcast_to(jnp.arange(n_rows, dtype=F32)[:, None], (n_rows, GRID_W)).reshape(-1)
    cols = jnp.broadcast_to(jnp.arange(GRID_W, dtype=F32)[None, :], (n_rows, GRID_W)).reshape(-1)
    n_freq = HEAD_DIM // 4
    freqs = ROPE_BASE ** (-jnp.arange(n_freq, dtype=F32) / n_freq)
    ang = jnp.concatenate([rows[:, None] * freqs, cols[:, None] * freqs], axis=-1)
    return jnp.cos(ang), jnp.sin(ang)


def kernel(x, c, ctx, c_ctx, ada_w, ada_b, norm_mix, norm_ffn, norm_final, even_w_in, ret_log_decay, sc_conv_w, even_w_out, conf_w_pw1, conf_b_pw1, conf_dw_w, conf_dw_b, conf_ln_g, conf_ln_b, conf_w_pw2, conf_b_pw2, ffn_w1, ffn_w3, ffn_w2):
    d = D_MODEL
    seq = x.shape[1]
    xs = x[0]
    row = lambda v: v.reshape(1, -1)

    cc8 = jnp.zeros((8, d), F32).at[0].set(c[0]).at[1].set(c_ctx)
    mods = _ada(cc8, ada_w, ada_b)
    cos, sin = _rope_tables(seq)
    mod_rows = lambda i: [row(mods[i, 0][n * d:(n + 1) * d]) for n in range(N_MOD)]

    sh1, sc1, g1, sh2, sc2, g2 = mod_rows(0)
    h = _normmod(xs, row(norm_mix[0]), sh1, sc1)
    mc = mods[0, 1]
    hc = _normmod(ctx[0], row(norm_mix[0]), row(mc[:d]), row(mc[d:2 * d]))
    w_in = even_w_in[0].astype(BF16)
    qkv, w_out, w1_0 = _mm_rope(h, w_in, cos, sin, cast_jobs=((even_w_out, 0), (ffn_w1, 0)))
    rest, w3_0, w2_0 = _mm_plain(h, w_in, 3 * D_RET, D_RET + 3 * D_SC, F32, 1024, 1024,
                                 cast_jobs=((ffn_w3, 0), (ffn_w2, 0)))
    kvc = _mm_plain(hc, w_in, D_RET, 2 * D_RET, F32, CTX_LEN, 1024)
    ld = jnp.broadcast_to(ret_log_decay[0][:, :, None, None], (2, N_HEADS, 8, LANES))
    ofwd = _retention(ld, qkv, kvc)
    ret = _retention(ld, qkv, kvc, ofwd, rest)
    sc = _short_conv(rest, sc_conv_w[0])
    xs = _mm_resid([ret, sc], w_out, g1, xs, tn=1024)
    h = _normmod(xs, row(norm_ffn[0]), sh2, sc2)
    u, w_pw1, w_pw2 = _mm_swiglu(h, w1_0, w3_0, cast_jobs=((conf_w_pw1, 0), (conf_w_pw2, 0)))
    xs = _mm_resid([u], w2_0, g2, xs, tm=512, tn=1024, w_resident=True)

    sh1, sc1, g1, sh2, sc2, g2 = mod_rows(1)
    h = _normmod(xs, row(norm_mix[1]), sh1, sc1)
    u, w1_1, w3_1 = _mm_glu(h, w_pw1, row(conf_b_pw1[0]), d, F32, cast_jobs=((ffn_w1, 1), (ffn_w3, 1)))
    v = _dwconv_ln_silu(u, conf_dw_w[0], row(conf_dw_b[0]), row(conf_ln_g[0]), row(conf_ln_b[0]))
    xs = _mm_resid([v], w_pw2, g1, xs, bias=row(conf_b_pw2[0]), tn=1024)
    h = _normmod(xs, row(norm_ffn[1]), sh2, sc2)
    u, w2_1 = _mm_swiglu(h, w1_1, w3_1, cast_jobs=((ffn_w2, 1),))
    xs = _mm_resid([u], w2_1, g2, xs, tm=512, tn=1024, w_resident=True)
    out = _rmsnorm(xs, row(norm_final))
    return out[None]
```
